```python
import jax, jax.numpy as jnp
from jax import lax
import numpy as np

D_MODEL = 1024
BATCH = 16
SEQ = 4096
DEPTH = 1

HEAD_DIM = 64
NSA_HEADS = 8
NSA_KV_GROUPS = 2
NSA_HPG = NSA_HEADS // NSA_KV_GROUPS
NSA_WIDTH = NSA_HEADS * HEAD_DIM
NSA_KV_WIDTH = NSA_KV_GROUPS * HEAD_DIM
CMP_LEN = 32
CMP_STRIDE = 16
CMP_HIDDEN = 2 * HEAD_DIM
SLC_BLOCK = 64
SLC_TOPN = 16
WINDOW = 512
MOBA_HEADS = 8
MOBA_WIDTH = MOBA_HEADS * HEAD_DIM
MOBA_BLOCK = 256
MOBA_TOPK = 3
TOTAL_HEADS = NSA_HEADS + MOBA_HEADS
Q_CHUNK = 32
RMS_EPS = 1e-6
COL_WIDTHS = (NSA_WIDTH, 6 * NSA_KV_WIDTH, 3 * NSA_HEADS, NSA_WIDTH, 3 * MOBA_WIDTH, MOBA_WIDTH, 2 * D_MODEL)
IN_COLS = NSA_WIDTH + 6 * NSA_KV_WIDTH + 3 * NSA_HEADS + NSA_WIDTH + 3 * MOBA_WIDTH + MOBA_WIDTH + 2 * D_MODEL

kernel_name = 'hybrid_nsa_moba_gated_block'


def rms_norm(x, g):
    xf = x.astype(jnp.float32)
    y = xf * lax.rsqrt(jnp.mean(xf * xf, axis=-1, keepdims=True) + RMS_EPS)
    return (y * g.astype(jnp.float32)).astype(x.dtype)


def alibi_slopes(n):
    return jnp.asarray((2.0 ** (-8.0 * np.arange(1, n + 1) / n)).astype(np.float32))


def masked_softmax(s, mask):
    s = jnp.where(mask, s, -jnp.inf)
    m = jnp.max(s, axis=-1, keepdims=True)
    m = jnp.where(jnp.isfinite(m), m, 0.0)
    p = jnp.where(mask, jnp.exp(s - m), 0.0)
    return p / jnp.maximum(jnp.sum(p, axis=-1, keepdims=True), 1e-30)


def cmp_to_slc_matrix(n_cmp, n_slc):
    cs = np.arange(n_cmp) * CMP_STRIDE
    ce = cs + CMP_LEN - 1
    ss = np.arange(n_slc) * SLC_BLOCK
    se = ss + SLC_BLOCK - 1
    return ((cs[:, None] <= se[None, :]) & (ce[:, None] >= ss[None, :])).astype(np.float32)


def compress(k, pe, w1, w2):
    S = k.shape[2]
    n_cmp = (S - CMP_LEN) // CMP_STRIDE + 1
    idx = np.arange(n_cmp)[:, None] * CMP_STRIDE + np.arange(CMP_LEN)[None, :]
    blocks = k[:, :, idx] + pe
    h = jax.nn.silu(jnp.einsum('bgnld,lde->bgne', blocks, w1))
    return jnp.einsum('bgne,ef->bgnf', h, w2)


def nsa_attention(q, kc, vc, ks, vs, kw, vw, gates, slopes):
    B, G, P, S, hd = q.shape
    n_cmp = kc.shape[2]
    n_slc = S // SLC_BLOCK
    k_sel = min(SLC_TOPN, n_slc)
    scale = hd ** -0.5
    cmp_end = jnp.asarray(np.arange(n_cmp) * CMP_STRIDE + CMP_LEN - 1, jnp.int32)
    overlap = jnp.asarray(cmp_to_slc_matrix(n_cmp, n_slc))
    ks_b = ks.reshape(B, G, n_slc, SLC_BLOCK, hd)
    vs_b = vs.reshape(B, G, n_slc, SLC_BLOCK, hd)
    kw_p = jnp.pad(kw, ((0, 0), (0, 0), (WINDOW, 0), (0, 0)))
    vw_p = jnp.pad(vw, ((0, 0), (0, 0), (WINDOW, 0), (0, 0)))
    bi = jnp.arange(B)[:, None, None, None]
    gi = jnp.arange(G)[None, :, None, None]
    blk = jnp.arange(n_slc)
    in_blk = jnp.arange(SLC_BLOCK)
    win_off = jnp.arange(WINDOW + Q_CHUNK) - WINDOW
    sl = slopes.astype(jnp.float32)[None, :, :, None, None]

    def chunk(ci):
        c0 = ci * Q_CHUNK
        t = c0 + jnp.arange(Q_CHUNK)
        qc = lax.dynamic_slice_in_dim(q, c0, Q_CHUNK, axis=3)
        g = lax.dynamic_slice_in_dim(gates, c0, Q_CHUNK, axis=4)
        dist = t[:, None] - cmp_end[None, :]
        s = jnp.einsum('bgpqd,bgnd->bgpqn', qc, kc).astype(jnp.float32) * scale - sl * dist.astype(jnp.float32)
        p_cmp = masked_softmax(s, dist >= 0)
        o_cmp = jnp.einsum('bgpqn,bgnd->bgpqd', p_cmp.astype(vc.dtype), vc)
        imp = jnp.einsum('bgpqn,nj->bgqj', p_cmp, overlap)
        cur = t // SLC_BLOCK
        causal_blk = blk[None, :] <= cur[:, None]
        forced = (blk[None, :] == 0) | (blk[None, :] == cur[:, None]) | (blk[None, :] == cur[:, None] - 1)
        imp = jnp.where(forced, jnp.inf, imp)
        imp = jnp.where(causal_blk, imp, -jnp.inf)
        _, sel = lax.top_k(imp, k_sel)
        kg = ks_b[bi, gi, sel]
        vg = vs_b[bi, gi, sel]
        pos = (sel[..., None] * SLC_BLOCK + in_blk).reshape(B, G, 1, Q_CHUNK, k_sel * SLC_BLOCK)
        dist = t[:, None] - pos
        s = jnp.einsum('bgpqd,bgqkld->bgpqkl', qc, kg).reshape(B, G, P, Q_CHUNK, k_sel * SLC_BLOCK)
        s = s.astype(jnp.float32) * scale - sl * dist.astype(jnp.float32)
        p = masked_softmax(s, dist >= 0).astype(vg.dtype).reshape(B, G, P, Q_CHUNK, k_sel, SLC_BLOCK)
        o_slc = jnp.einsum('bgpqkl,bgqkld->bgpqd', p, vg)
        kwin = lax.dynamic_slice_in_dim(kw_p, c0, WINDOW + Q_CHUNK, axis=2)
        vwin = lax.dynamic_slice_in_dim(vw_p, c0, WINDOW + Q_CHUNK, axis=2)
        posw = c0 + win_off
        dist = t[:, None] - posw[None, :]
        mask = (posw[None, :] >= 0) & (dist >= 0) & (dist < WINDOW)
        s = jnp.einsum('bgpqd,bgkd->bgpqk', qc, kwin).astype(jnp.float32) * scale - sl * dist.astype(jnp.float32)
        p = masked_softmax(s, mask).astype(vwin.dtype)
        o_win = jnp.einsum('bgpqk,bgkd->bgpqd', p, vwin)
        return g[0][..., None] * o_cmp + g[1][..., None] * o_slc + g[2][..., None] * o_win

    out = lax.map(chunk, jnp.arange(S // Q_CHUNK))
    return out.transpose(1, 0, 4, 2, 3, 5).reshape(B, S, G * P * hd)


def moba_attention(q, k, v, slopes):
    B, H, S, hd = q.shape
    nb = -(-S // MOBA_BLOCK)
    pad = nb * MOBA_BLOCK - S
    kb = jnp.pad(k, ((0, 0), (0, 0), (0, pad), (0, 0))).reshape(B, H, nb, MOBA_BLOCK, hd)
    vb = jnp.pad(v, ((0, 0), (0, 0), (0, pad), (0, 0))).reshape(B, H, nb, MOBA_BLOCK, hd)
    kbar = jnp.mean(kb.astype(jnp.float32), axis=3)
    k_sel = min(MOBA_TOPK, nb)
    scale = hd ** -0.5
    bi = jnp.arange(B)[:, None, None, None]
    hi = jnp.arange(H)[None, :, None, None]
    offs = jnp.arange(MOBA_BLOCK)
    blocks = jnp.arange(nb)
    sl = slopes.astype(jnp.float32)[None, :, None, None]

    def chunk(ci):
        c0 = ci * Q_CHUNK
        t = c0 + jnp.arange(Q_CHUNK)
        cur = c0 // MOBA_BLOCK
        qc = lax.dynamic_slice_in_dim(q, c0, Q_CHUNK, axis=2)
        gate = jnp.einsum('bhqd,bhnd->bhqn', qc.astype(jnp.float32), kbar)
        gate = jnp.where(blocks < cur, gate, -jnp.inf)
        _, sel = lax.top_k(gate, k_sel)
        valid = sel < cur
        kg = kb[bi, hi, sel]
        vg = vb[bi, hi, sel]
        ko = lax.dynamic_index_in_dim(kb, cur, axis=2, keepdims=False)
        vo = lax.dynamic_index_in_dim(vb, cur, axis=2, keepdims=False)
        n_sel = k_sel * MOBA_BLOCK
        s_sel = jnp.einsum('bhqd,bhqkld->bhqkl', qc, kg).reshape(B, H, Q_CHUNK, n_sel)
        s_own = jnp.einsum('bhqd,bhld->bhql', qc, ko)
        s = jnp.concatenate([s_sel, s_own], axis=-1).astype(jnp.float32) * scale
        pos_sel = (sel[..., None] * MOBA_BLOCK + offs).reshape(B, H, Q_CHUNK, n_sel)
        pos_own = jnp.broadcast_to(cur * MOBA_BLOCK + offs, (B, H, Q_CHUNK, MOBA_BLOCK))
        pos = jnp.concatenate([pos_sel, pos_own], axis=-1)
        dist = t[:, None] - pos
        mask_sel = jnp.broadcast_to(valid[..., None], (B, H, Q_CHUNK, k_sel, MOBA_BLOCK)).reshape(B, H, Q_CHUNK, n_sel)
        mask = jnp.concatenate([mask_sel, dist[..., n_sel:] >= 0], axis=-1)
        s = s - sl * dist.astype(jnp.float32)
        p = masked_softmax(s, mask).astype(v.dtype)
        p_sel = p[..., :n_sel].reshape(B, H, Q_CHUNK, k_sel, MOBA_BLOCK)
        p_own = p[..., n_sel:]
        return jnp.einsum('bhqkl,bhqkld->bhqd', p_sel, vg) + jnp.einsum('bhql,bhld->bhqd', p_own, vo)

    out = lax.map(chunk, jnp.arange(S // Q_CHUNK))
    return out.transpose(1, 0, 3, 2, 4).reshape(B, S, H * hd)


def hybrid_layer(x, norm_pre, w_in, pe_k, w1_k, w2_k, pe_v, w1_v, w2_v, w_a, w_b, w_o, norm_post):
    B, S, _ = x.shape
    G, P, hd = NSA_KV_GROUPS, NSA_HPG, HEAD_DIM
    u = rms_norm(x, norm_pre)
    proj = u @ w_in
    splits = [int(c) for c in np.cumsum(COL_WIDTHS)[:-1]]
    q_a, kv_a, g_a, z_a, qkv_b, z_b, merge = jnp.split(proj, splits, axis=-1)
    slopes = alibi_slopes(TOTAL_HEADS)
    q_a = q_a.reshape(B, S, G, P, hd).transpose(0, 2, 3, 1, 4)
    kv_a = kv_a.reshape(B, S, 6, G, hd).transpose(2, 0, 3, 1, 4)
    kc = compress(kv_a[0], pe_k, w1_k, w2_k)
    vc = compress(kv_a[1], pe_v, w1_v, w2_v)
    gates_a = jax.nn.sigmoid(g_a).reshape(B, S, 3, G, P).transpose(2, 0, 3, 4, 1)
    o_a = nsa_attention(q_a, kc, vc, kv_a[2], kv_a[3], kv_a[4], kv_a[5], gates_a,
                        slopes[0::2].reshape(G, P))
    y_a = o_a * jax.nn.silu(z_a)
    qkv_b = qkv_b.reshape(B, S, 3, MOBA_HEADS, hd).transpose(2, 0, 3, 1, 4)
    o_b = moba_attention(qkv_b[0], qkv_b[1], qkv_b[2], slopes[1::2])
    y_b = o_b * jax.nn.silu(z_b)
    gate_a, gate_b = jnp.split(merge, 2, axis=-1)
    m = jax.nn.sigmoid(gate_a) * (y_a @ w_a) + jax.nn.sigmoid(gate_b) * (y_b @ w_b)
    return x + rms_norm(m @ w_o, norm_post)


def setup_inputs(seed: int = 0) -> dict:
    key = jax.random.key(seed)
    ks = jax.random.split(key, 13)

    def nrm(k, shape, s):
        return jax.random.normal(k, shape, jnp.float32) * s

    return {
        'x': nrm(ks[0], (BATCH, SEQ, D_MODEL), 1.0),
        'norm_pre': 1.0 + nrm(ks[1], (DEPTH, D_MODEL), 0.1),
        'w_in': nrm(ks[2], (DEPTH, D_MODEL, IN_COLS), D_MODEL ** -0.5),
        'cmp_pe_k': nrm(ks[3], (DEPTH, CMP_LEN, HEAD_DIM), 0.1),
        'cmp_w1_k': nrm(ks[4], (DEPTH, CMP_LEN, HEAD_DIM, CMP_HIDDEN), (CMP_LEN * HEAD_DIM) ** -0.5),
        'cmp_w2_k': nrm(ks[5], (DEPTH, CMP_HIDDEN, HEAD_DIM), CMP_HIDDEN ** -0.5),
        'cmp_pe_v': nrm(ks[6], (DEPTH, CMP_LEN, HEAD_DIM), 0.1),
        'cmp_w1_v': nrm(ks[7], (DEPTH, CMP_LEN, HEAD_DIM, CMP_HIDDEN), (CMP_LEN * HEAD_DIM) ** -0.5),
        'cmp_w2_v': nrm(ks[8], (DEPTH, CMP_HIDDEN, HEAD_DIM), CMP_HIDDEN ** -0.5),
        'w_branch_a': nrm(ks[9], (DEPTH, NSA_WIDTH, D_MODEL), NSA_WIDTH ** -0.5),
        'w_branch_b': nrm(ks[10], (DEPTH, MOBA_WIDTH, D_MODEL), MOBA_WIDTH ** -0.5),
        'w_o': nrm(ks[11], (DEPTH, D_MODEL, D_MODEL), D_MODEL ** -0.5),
        'norm_post': 1.0 + nrm(ks[12], (DEPTH, D_MODEL), 0.1),
    }


def reference(x, norm_pre, w_in, cmp_pe_k, cmp_w1_k, cmp_w2_k, cmp_pe_v, cmp_w1_v, cmp_w2_v,
              w_branch_a, w_branch_b, w_o, norm_post):
    h = x
    for l in range(DEPTH):
        h = hybrid_layer(h, norm_pre[l], w_in[l], cmp_pe_k[l], cmp_w1_k[l], cmp_w2_k[l],
                         cmp_pe_v[l], cmp_w1_v[l], cmp_w2_v[l], w_branch_a[l], w_branch_b[l],
                         w_o[l], norm_post[l])
    return h
```

```python
import functools

import numpy as np
import jax
import jax.numpy as jnp
from jax import lax
from jax.experimental import pallas as pl
from jax.experimental.pallas import tpu as pltpu

F32 = jnp.float32
BF16 = jnp.bfloat16

HEAD_DIM = 64
NSA_HEADS = 8
NSA_GROUPS = 2
NSA_HPG = NSA_HEADS // NSA_GROUPS
CMP_LEN = 32
CMP_STRIDE = 16
CMP_HIDDEN = 2 * HEAD_DIM
SLC_BLOCK = 64
SLC_TOPN = 16
WINDOW = 512
MOBA_HEADS = 8
MOBA_BLOCK = 256
MOBA_TOPK = 3
TOTAL_HEADS = NSA_HEADS + MOBA_HEADS
RMS_EPS = 1e-6

TILE = 256
V_ROWS = 80
Q_ROWS = 256
K_COLS = 128
NEG = -1e30
LOG2E = float(np.log2(np.e))
QSCALE = float(HEAD_DIM ** -0.5) * LOG2E
VMEM_LIMIT = 56 * 1024 * 1024


def _split3(v):
    v = np.asarray(v, np.float32)
    rnd = lambda a: a.astype(BF16).astype(np.float32)
    hi = rnd(v)
    mid = rnd(v - hi)
    lo = rnd(v - hi - mid)
    return hi, mid, lo


def _alibi_slopes():
    s = (2.0 ** (-8.0 * np.arange(1, TOTAL_HEADS + 1) / TOTAL_HEADS)).astype(np.float32)
    return np.concatenate([s[0::2], s[1::2]])


def _q_aug_rows(width):
    sl = (_alibi_slopes().astype(np.float64) * LOG2E).astype(np.float32)
    hi, mid, lo = _split3(sl)
    a = np.zeros((TOTAL_HEADS, HEAD_DIM, width), np.float32)
    for r, part in enumerate((hi, hi, mid, mid, lo, lo)):
        a[:, r, :] = part[:, None]
    return jnp.asarray(a, BF16)


def _pos_aug_cols(pos):
    pos = np.asarray(pos, np.int64)
    p_hi = (pos // 64 * 64).astype(np.float32)
    p_lo = (pos % 64).astype(np.float32)
    a = np.zeros((pos.shape[0], HEAD_DIM), np.float32)
    for c in range(3):
        a[:, 2 * c] = p_hi
        a[:, 2 * c + 1] = p_lo
    return jnp.asarray(a)


def _sigmoid(x):
    return 1.0 / (1.0 + jnp.exp(-x))


def _hi_lo(x):
    hi = x.astype(BF16)
    lo = (x - hi.astype(F32)).astype(BF16)
    return hi, lo


def _dot(a, b):
    return jnp.dot(a, b, preferred_element_type=F32)


def _dot_nt(a, b):
    return lax.dot_general(a, b, (((1,), (1,)), ((), ())), preferred_element_type=F32)


def _dot3(a, b):
    a_hi, a_lo = _hi_lo(a)
    b_hi, b_lo = _hi_lo(b)
    return _dot(a_hi, b_hi) + _dot(a_hi, b_lo) + _dot(a_lo, b_hi)


N_VT = 12
N_KA = 12
CT_Q = 0
CT_V = 1024
CT_Z = CT_V + N_VT * HEAD_DIM
CT_G = CT_Z + 1024
CT_ROWS = CT_G + 32
CN_KCV = 0
CN_KA = 256
CN_MERGE = CN_KA + N_KA * HEAD_DIM


def _proj_kernel(x_ref, g_ref, wt_ref, wn_ref, qaug_ref, pos_ref,
                 qa_ref, qb_ref, vt_ref, sz_ref, gt_ref, kcv_ref, kaug_ref, kbar_ref, sgm_ref):
    x = x_ref[0]
    tr = x.shape[0]
    ms = jnp.mean(x * x, axis=-1, keepdims=True)
    u = (x * lax.rsqrt(ms + RMS_EPS) * g_ref[...]).astype(BF16)

    acc_t = _dot_nt(wt_ref[...], u)
    for h in range(TOTAL_HEADS):
        q = acc_t[h * HEAD_DIM:(h + 1) * HEAD_DIM] * QSCALE
        hi, lo = _hi_lo(q)
        ref = qa_ref if h < NSA_HEADS else qb_ref
        hh = h % NSA_HEADS
        ref[0, hh, 0:64, :] = hi
        ref[0, hh, 64:128, :] = qaug_ref[h]
        ref[0, hh, 128:192, :] = lo
        ref[0, hh, 192:256, :] = hi
    ones_rows = (lax.broadcasted_iota(jnp.int32, (V_ROWS - HEAD_DIM, tr), 0) == 0).astype(F32)
    for j in range(N_VT):
        v = acc_t[CT_V + j * HEAD_DIM:CT_V + (j + 1) * HEAD_DIM]
        vt_ref[0, j, 0] = jnp.concatenate([v, ones_rows], axis=0).astype(BF16)
    z = acc_t[CT_Z:CT_G]
    sz_ref[0] = (z * _sigmoid(z)).astype(BF16)
    gt_ref[0] = _sigmoid(acc_t[CT_G:CT_ROWS])

    acc_n = _dot(u, wn_ref[...])
    for j in range(4):
        kcv_ref[0, j] = acc_n[:, CN_KCV + j * HEAD_DIM:CN_KCV + (j + 1) * HEAD_DIM]
    pos = pos_ref[...]
    for j in range(N_KA):
        k = acc_n[:, CN_KA + j * HEAD_DIM:CN_KA + (j + 1) * HEAD_DIM]
        kaug_ref[0, j] = jnp.concatenate([k, pos], axis=1).astype(BF16)
    kb = acc_n[:, CN_KA + 4 * HEAD_DIM:CN_MERGE]
    kbar_ref[0, 0] = jnp.mean(kb, axis=0, keepdims=True)
    sgm_ref[0] = _sigmoid(acc_n[:, CN_MERGE:]).astype(BF16)


def _proj_weights(w_in, d_model):
    o_q = 0
    o_kv = o_q + NSA_HEADS * HEAD_DIM
    o_g = o_kv + 6 * NSA_GROUPS * HEAD_DIM
    o_za = o_g + 3 * NSA_HEADS
    o_b = o_za + NSA_HEADS * HEAD_DIM
    o_zb = o_b + 3 * MOBA_HEADS * HEAD_DIM
    o_m = o_zb + MOBA_HEADS * HEAD_DIM
    kv = lambda j: w_in[:, o_kv + j * 128:o_kv + (j + 1) * 128]
    qb = w_in[:, o_b:o_b + 512]
    kb = w_in[:, o_b + 512:o_b + 1024]
    vb = w_in[:, o_b + 1024:o_b + 1536]
    wt = jnp.concatenate([
        w_in[:, o_q:o_kv], qb, kv(3), kv(5), vb,
        w_in[:, o_za:o_b], w_in[:, o_zb:o_m], w_in[:, o_g:o_za],
        jnp.zeros((d_model, CT_ROWS - CT_G - 3 * NSA_HEADS), w_in.dtype)], axis=1)
    wn = jnp.concatenate([kv(0), kv(1), kv(2), kv(4), kb, w_in[:, o_m:]], axis=1)
    return wt.T.astype(BF16), wn.astype(BF16)


def _project(x, norm_pre, w_in):
    b, s, d = x.shape
    nt = s // TILE
    wt, wn = _proj_weights(w_in, d)
    cn = wn.shape[1]
    qaug = _q_aug_rows(TILE)
    pos = _pos_aug_cols(np.arange(s))
    const = lambda *shape: pl.BlockSpec(shape, lambda bi, r: (0,) * len(shape))
    out_shape = (
        jax.ShapeDtypeStruct((b, NSA_HEADS, Q_ROWS, s), BF16),
        jax.ShapeDtypeStruct((b, MOBA_HEADS, Q_ROWS, s), BF16),
        jax.ShapeDtypeStruct((b, N_VT, nt, V_ROWS, TILE), BF16),
        jax.ShapeDtypeStruct((b, 1024, s), BF16),
        jax.ShapeDtypeStruct((b, 32, s), F32),
        jax.ShapeDtypeStruct((b, 4, s, HEAD_DIM), F32),
        jax.ShapeDtypeStruct((b, N_KA, s, K_COLS), BF16),
        jax.ShapeDtypeStruct((b, nt, 1, MOBA_HEADS * HEAD_DIM), F32),
        jax.ShapeDtypeStruct((b, s, 2 * d), BF16),
    )
    out_specs = (
        pl.BlockSpec((1, NSA_HEADS, Q_ROWS, TILE), lambda bi, r: (bi, 0, 0, r)),
        pl.BlockSpec((1, MOBA_HEADS, Q_ROWS, TILE), lambda bi, r: (bi, 0, 0, r)),
        pl.BlockSpec((1, N_VT, 1, V_ROWS, TILE), lambda bi, r: (bi, 0, r, 0, 0)),
        pl.BlockSpec((1, 1024, TILE), lambda bi, r: (bi, 0, r)),
        pl.BlockSpec((1, 32, TILE), lambda bi, r: (bi, 0, r)),
        pl.BlockSpec((1, 4, TILE, HEAD_DIM), lambda bi, r: (bi, 0, r, 0)),
        pl.BlockSpec((1, N_KA, TILE, K_COLS), lambda bi, r: (bi, 0, r, 0)),
        pl.BlockSpec((1, 1, 1, MOBA_HEADS * HEAD_DIM), lambda bi, r: (bi, r, 0, 0)),
        pl.BlockSpec((1, TILE, 2 * d), lambda bi, r: (bi, r, 0)),
    )
    return pl.pallas_call(
        _proj_kernel,
        grid=(b, nt),
        in_specs=[
            pl.BlockSpec((1, TILE, d), lambda bi, r: (bi, r, 0)),
            const(1, d),
            const(CT_ROWS, d),
            const(d, cn),
            const(TOTAL_HEADS, HEAD_DIM, TILE),
            pl.BlockSpec((TILE, HEAD_DIM), lambda bi, r: (r, 0)),
        ],
        out_specs=out_specs,
        out_shape=out_shape,
        compiler_params=pltpu.CompilerParams(vmem_limit_bytes=VMEM_LIMIT),
        name="proj",
    )(x, norm_pre.reshape(1, d), wt, wn, qaug, pos)


def _compress_core(r_ref, pe_ref, w1_ref, w2_ref):
    r = r_ref[0, 0]
    nr = r.shape[0]
    a = _dot3(r + pe_ref[0], w1_ref[0])
    bm = _dot3(r + pe_ref[1], w1_ref[1])
    h = a + pltpu.roll(bm, nr - 1, axis=0)
    h = h * _sigmoid(h)
    c = _dot3(h, w2_ref[...])
    valid = lax.broadcasted_iota(jnp.int32, c.shape, 0) < nr - 1
    return jnp.where(valid, c, 0.0)


def _compress_k_kernel(r_ref, pe_ref, w1_ref, w2_ref, pos_ref, o_ref):
    c = _compress_core(r_ref, pe_ref, w1_ref, w2_ref)
    hi, lo = _hi_lo(c)
    hi = hi.astype(F32)
    o_ref[0, 0] = jnp.concatenate([hi, pos_ref[...], hi, lo.astype(F32)], axis=1).astype(BF16)


def _compress_v_kernel(r_ref, pe_ref, w1_ref, w2_ref, o_ref):
    c = _compress_core(r_ref, pe_ref, w1_ref, w2_ref)
    o_ref[0, 0] = c.T.astype(BF16)


def _compress(kcv, pe_k, w1_k, w2_k, pe_v, w1_v, w2_v):
    b, _, s, hd = kcv.shape
    nr = s // CMP_STRIDE
    half = CMP_STRIDE * hd
    r = kcv.reshape(b, 4, nr, half)
    prep = lambda pe, w1: (pe.reshape(2, 1, half), w1.reshape(2, half, CMP_HIDDEN))
    pos = _pos_aug_cols(np.arange(nr) * CMP_STRIDE + CMP_LEN - 1)
    const = lambda *shape: pl.BlockSpec(shape, lambda bi, g: (0,) * len(shape))
    common = [const(2, 1, half), const(2, half, CMP_HIDDEN), const(CMP_HIDDEN, hd)]
    pe, w1 = prep(pe_k, w1_k)
    kc = pl.pallas_call(
        _compress_k_kernel,
        grid=(b, NSA_GROUPS),
        in_specs=[pl.BlockSpec((1, 1, nr, half), lambda bi, g: (bi, g, 0, 0))] + common + [const(nr, hd)],
        out_specs=pl.BlockSpec((1, 1, nr, 4 * hd), lambda bi, g: (bi, g, 0, 0)),
        out_shape=jax.ShapeDtypeStruct((b, NSA_GROUPS, nr, 4 * hd), BF16),
        name="compress_k",
    )(r, pe, w1, w2_k, pos)
    pe, w1 = prep(pe_v, w1_v)
    vc = pl.pallas_call(
        _compress_v_kernel,
        grid=(b, NSA_GROUPS),
        in_specs=[pl.BlockSpec((1, 1, nr, half), lambda bi, g: (bi, g + NSA_GROUPS, 0, 0))] + common,
        out_specs=pl.BlockSpec((1, 1, hd, nr), lambda bi, g: (bi, g, 0, 0)),
        out_shape=jax.ShapeDtypeStruct((b, NSA_GROUPS, hd, nr), BF16),
        name="compress_v",
    )(r, pe, w1, w2_v)
    return kc, vc


def _rank_select(v, n_rows, topn):
    jidx = lax.broadcasted_iota(jnp.int32, v.shape, 0)
    cnt = jnp.zeros(v.shape, F32)
    for i in range(n_rows):
        vi = v[i:i + 1, :]
        ge = jnp.where(vi >= v, 1.0, 0.0)
        gt = jnp.where(vi > v, 1.0, 0.0)
        cnt = cnt + jnp.where(jidx > i, ge, gt)
    return cnt < float(topn)


def _nsa_cmp_kernel(q_ref, kc_ref, vc_ref, g_ref, ov_ref, ocmp_ref, selb_ref):
    i = pl.program_id(2)
    g = pl.program_id(1)
    kc = kc_ref[0, 0]
    vc = vc_ref[0, 0]
    nc = kc.shape[0]
    tq = q_ref.shape[-1]
    n_idx = lax.broadcasted_iota(jnp.int32, (nc, tq), 0)
    t = lax.broadcasted_iota(jnp.int32, (nc, tq), 1) + i * tq
    mask = t >= n_idx * CMP_STRIDE + (CMP_LEN - 1)
    psum = jnp.zeros((nc, tq), F32)
    for p in range(NSA_HPG):
        s = _dot(kc, q_ref[0, p])
        s = jnp.where(mask, s, NEG)
        m = jnp.max(s, axis=0, keepdims=True)
        e = jnp.where(mask, jnp.exp2(s - m), 0.0)
        l = jnp.sum(e, axis=0, keepdims=True)
        pn = e / jnp.maximum(l, 1e-30)
        o = _dot(vc, pn.astype(BF16))
        gate = g_ref[0, pl.ds(g * NSA_HPG + p, 1), :]
        ocmp_ref[0, p] = o * gate
        psum = psum + pn
    p_hi, p_lo = _hi_lo(psum)
    imp = _dot(ov_ref[...], p_hi) + _dot(ov_ref[...], p_lo)
    n_slc = imp.shape[0]
    blk = lax.broadcasted_iota(jnp.int32, (n_slc, tq), 0)
    cur = (lax.broadcasted_iota(jnp.int32, (n_slc, tq), 1) + i * tq) // SLC_BLOCK
    forced = (blk == 0) | (blk == cur) | (blk == cur - 1)
    causal = blk <= cur
    v = jnp.where(forced, 3e38, imp)
    v = jnp.where(causal, v, NEG)
    sel = _rank_select(v, n_slc, SLC_TOPN)
    selb_ref[0, 0] = jnp.where(sel & causal, 0.0, NEG)


def _nsa_cmp(qa, kc, vc, gt):
    b, _, _, s = qa.shape
    nq = s // TILE
    nc = kc.shape[2]
    n_slc = s // SLC_BLOCK
    cs = np.arange(nc) * CMP_STRIDE
    ss = np.arange(n_slc) * SLC_BLOCK
    ov = ((cs[None, :] <= ss[:, None] + SLC_BLOCK - 1) & (cs[None, :] + CMP_LEN - 1 >= ss[:, None]))
    ov[:, nc - 1] = False
    ov = jnp.asarray(ov.astype(np.float32), BF16)
    return pl.pallas_call(
        _nsa_cmp_kernel,
        grid=(b, NSA_GROUPS, nq),
        in_specs=[
            pl.BlockSpec((1, NSA_HPG, Q_ROWS, TILE), lambda bi, g, i: (bi, g, 0, i)),
            pl.BlockSpec((1, 1, nc, 4 * HEAD_DIM), lambda bi, g, i: (bi, g, 0, 0)),
            pl.BlockSpec((1, 1, HEAD_DIM, nc), lambda bi, g, i: (bi, g, 0, 0)),
            pl.BlockSpec((1, 32, TILE), lambda bi, g, i: (bi, 0, i)),
            pl.BlockSpec((n_slc, nc), lambda bi, g, i: (0, 0)),
        ],
        out_specs=(
            pl.BlockSpec((1, NSA_HPG, HEAD_DIM, TILE), lambda bi, g, i: (bi, g, 0, i)),
            pl.BlockSpec((1, 1, n_slc, TILE), lambda bi, g, i: (bi, g, 0, i)),
        ),
        out_shape=(
            jax.ShapeDtypeStruct((b, NSA_HEADS, HEAD_DIM, s), F32),
            jax.ShapeDtypeStruct((b, NSA_GROUPS, n_slc, s), F32),
        ),
        name="nsa_cmp",
    )(qa, kc, vc, gt, ov)


def _attn_step(k_tile, v_tile, q, bias, m_ref, acc_ref, idx, first):
    s = _dot(k_tile, q)
    if bias is not None:
        s = s + bias
    mt = jnp.max(s, axis=0, keepdims=True)
    if first:
        m_new = mt
        p = jnp.exp2(s - m_new).astype(BF16)
        acc = _dot(v_tile, p)
    else:
        m_old = m_ref[idx]
        m_new = jnp.maximum(m_old, mt)
        p = jnp.exp2(s - m_new).astype(BF16)
        acc = jnp.exp2(m_old - m_new) * acc_ref[idx] + _dot(v_tile, p)
    m_ref[idx] = m_new
    acc_ref[idx] = acc


def _finish(acc_ref, idx):
    acc = acc_ref[idx]
    return acc[0:HEAD_DIM] / acc[HEAD_DIM:HEAD_DIM + 1]


def _tri_biases(tk, tq):
    r = lax.broadcasted_iota(jnp.int32, (tk, tq), 0)
    c = lax.broadcasted_iota(jnp.int32, (tk, tq), 1)
    causal = jnp.where(r > c, NEG, 0.0)
    far = jnp.where(r > c, 0.0, NEG)
    return causal, far


def _nsa_attn_kernel(q_ref, ksl_ref, vsl_ref, kwn_ref, vwn_ref, selb_ref, g_ref, ocmp_ref,
                     o_ref, m_ref, acc_ref):
    i = pl.program_id(2)
    g = pl.program_id(1)
    tq = q_ref.shape[-1]
    tk = TILE
    causal, far = _tri_biases(tk, tq)
    nblk = tk // SLC_BLOCK

    def sel_bias(kt):
        rows = [jnp.broadcast_to(selb_ref[0, 0, pl.ds(kt * nblk + j, 1), :], (SLC_BLOCK, tq))
                for j in range(nblk)]
        return jnp.concatenate(rows, axis=0)

    def k_rows(ref, kt):
        return ref[0, 0, pl.ds(pl.multiple_of(kt * tk, tk), tk), :]

    def tile(kref, vref, kt, bias, base, first):
        k_tile = k_rows(kref, kt)
        v_tile = vref[0, 0, kt]
        for p in range(NSA_HPG):
            _attn_step(k_tile, v_tile, q_ref[0, p, 0:K_COLS, :], bias, m_ref, acc_ref, base + p, first)

    tile(ksl_ref, vsl_ref, i, sel_bias(i) + causal, 0, True)

    def slc_body(kt, carry):
        tile(ksl_ref, vsl_ref, kt, sel_bias(kt), 0, False)
        return carry
    lax.fori_loop(0, i, slc_body, 0)

    tile(kwn_ref, vwn_ref, i, causal, NSA_HPG, True)

    @pl.when(i >= 1)
    def _():
        tile(kwn_ref, vwn_ref, i - 1, None, NSA_HPG, False)

    @pl.when(i >= 2)
    def _():
        tile(kwn_ref, vwn_ref, i - 2, far, NSA_HPG, False)

    for p in range(NSA_HPG):
        row = g * NSA_HPG + p
        g_slc = g_ref[0, pl.ds(NSA_HEADS + row, 1), :]
        g_win = g_ref[0, pl.ds(2 * NSA_HEADS + row, 1), :]
        o_ref[0, p] = ocmp_ref[0, p] + g_slc * _finish(acc_ref, p) + g_win * _finish(acc_ref, NSA_HPG + p)


def _nsa_attn(qa, kaug, vt, selb, gt, ocmp):
    b, _, _, s = qa.shape
    nq = s // TILE
    n_slc = s // SLC_BLOCK
    assert WINDOW == 2 * TILE and TILE % SLC_BLOCK == 0
    kspec = lambda off: pl.BlockSpec((1, 1, s, K_COLS), lambda bi, g, i: (bi, g + off, 0, 0))
    vspec = lambda off: pl.BlockSpec((1, 1, nq, V_ROWS, TILE), lambda bi, g, i: (bi, g + off, 0, 0, 0))
    return pl.pallas_call(
        _nsa_attn_kernel,
        grid=(b, NSA_GROUPS, nq),
        in_specs=[
            pl.BlockSpec((1, NSA_HPG, Q_ROWS, TILE), lambda bi, g, i: (bi, g, 0, i)),
            kspec(0), vspec(0), kspec(NSA_GROUPS), vspec(NSA_GROUPS),
            pl.BlockSpec((1, 1, n_slc, TILE), lambda bi, g, i: (bi, g, 0, i)),
            pl.BlockSpec((1, 32, TILE), lambda bi, g, i: (bi, 0, i)),
            pl.BlockSpec((1, NSA_HPG, HEAD_DIM, TILE), lambda bi, g, i: (bi, g, 0, i)),
        ],
        out_specs=pl.BlockSpec((1, NSA_HPG, HEAD_DIM, TILE), lambda bi, g, i: (bi, g, 0, i)),
        out_shape=jax.ShapeDtypeStruct((b, NSA_HEADS, HEAD_DIM, s), F32),
        scratch_shapes=[pltpu.VMEM((2 * NSA_HPG, 1, TILE), F32),
                        pltpu.VMEM((2 * NSA_HPG, V_ROWS, TILE), F32)],
        compiler_params=pltpu.CompilerParams(vmem_limit_bytes=VMEM_LIMIT),
        name="nsa_attn",
    )(qa, kaug, vt, kaug, vt, selb, gt, ocmp)


def _moba_kernel(q_ref, k_ref, v_ref, kbar_ref, o_ref, m_ref, acc_ref, sb_ref):
    i = pl.program_id(2)
    tq = q_ref.shape[-1]
    tk = TILE
    causal, _ = _tri_biases(tk, tq)

    kb_hi, kb_lo = _hi_lo(kbar_ref[0, 0])
    gate = (_dot(kb_hi, q_ref[0, 0, 0:64, :]) + _dot(kb_hi, q_ref[0, 0, 128:192, :])
            + _dot(kb_lo, q_ref[0, 0, 192:256, :]))
    nb = gate.shape[0]
    past = lax.broadcasted_iota(jnp.int32, (nb, tq), 0) < i
    sel = _rank_select(jnp.where(past, gate, NEG), nb, MOBA_TOPK)
    sb_ref[...] = jnp.where(sel & past, 0.0, NEG)

    q = q_ref[0, 0, 0:K_COLS, :]
    k_own = k_ref[0, 0, pl.ds(pl.multiple_of(i * tk, tk), tk), :]
    _attn_step(k_own, v_ref[0, 0, i], q, causal, m_ref, acc_ref, 0, True)

    def body(kt, carry):
        k_tile = k_ref[0, 0, pl.ds(pl.multiple_of(kt * tk, tk), tk), :]
        _attn_step(k_tile, v_ref[0, 0, kt], q, sb_ref[pl.ds(kt, 1), :], m_ref, acc_ref, 0, False)
        return carry
    lax.fori_loop(0, i, body, 0)

    o_ref[0, 0] = _finish(acc_ref, 0)


def _moba(qb, kaug, vt, kbar):
    b, _, _, s = qb.shape
    nq = s // TILE
    nb = kbar.shape[2]
    off = 2 * NSA_GROUPS
    return pl.pallas_call(
        _moba_kernel,
        grid=(b, MOBA_HEADS, nq),
        in_specs=[
            pl.BlockSpec((1, 1, Q_ROWS, TILE), lambda bi, h, i: (bi, h, 0, i)),
            pl.BlockSpec((1, 1, s, K_COLS), lambda bi, h, i: (bi, h + off, 0, 0)),
            pl.BlockSpec((1, 1, nq, V_ROWS, TILE), lambda bi, h, i: (bi, h + off, 0, 0, 0)),
            pl.BlockSpec((1, 1, nb, HEAD_DIM), lambda bi, h, i: (bi, h, 0, 0)),
        ],
        out_specs=pl.BlockSpec((1, 1, HEAD_DIM, TILE), lambda bi, h, i: (bi, h, 0, i)),
        out_shape=jax.ShapeDtypeStruct((b, MOBA_HEADS, HEAD_DIM, s), F32),
        scratch_shapes=[pltpu.VMEM((1, 1, TILE), F32),
                        pltpu.VMEM((1, V_ROWS, TILE), F32),
                        pltpu.VMEM((nb, TILE), F32)],
        name="moba",
    )(qb, kaug, vt, kbar)


def _out_kernel(oa_ref, ob_ref, sz_ref, sgm_ref, x_ref, wa_ref, wb_ref, wo_ref, g_ref, o_ref):
    d = x_ref.shape[-1]
    w = NSA_HEADS * HEAD_DIM
    ya = (oa_ref[0] * sz_ref[0, 0:w, :].astype(F32)).T.astype(BF16)
    yb = (ob_ref[0] * sz_ref[0, w:, :].astype(F32)).T.astype(BF16)
    sg = sgm_ref[0].astype(F32)
    m = sg[:, 0:d] * _dot(ya, wa_ref[...]) + sg[:, d:] * _dot(yb, wb_ref[...])
    r = _dot(m.astype(BF16), wo_ref[...])
    ms = jnp.mean(r * r, axis=-1, keepdims=True)
    o_ref[0] = x_ref[0] + r * lax.rsqrt(ms + RMS_EPS) * g_ref[...]


def _output(oa, ob, sz, sgm, x, w_a, w_b, w_o, norm_post):
    b, s, d = x.shape
    nt = s // TILE
    w = oa.shape[1]
    const = lambda *shape: pl.BlockSpec(shape, lambda bi, r: (0,) * len(shape))
    return pl.pallas_call(
        _out_kernel,
        grid=(b, nt),
        in_specs=[
            pl.BlockSpec((1, w, TILE), lambda bi, r: (bi, 0, r)),
            pl.BlockSpec((1, w, TILE), lambda bi, r: (bi, 0, r)),
            pl.BlockSpec((1, 2 * w, TILE), lambda bi, r: (bi, 0, r)),
            pl.BlockSpec((1, TILE, 2 * d), lambda bi, r: (bi, r, 0)),
            pl.BlockSpec((1, TILE, d), lambda bi, r: (bi, r, 0)),
            const(w, d), const(w, d), const(d, d), const(1, d),
        ],
        out_specs=pl.BlockSpec((1, TILE, d), lambda bi, r: (bi, r, 0)),
        out_shape=jax.ShapeDtypeStruct((b, s, d), F32),
        compiler_params=pltpu.CompilerParams(vmem_limit_bytes=VMEM_LIMIT),
        name="out_proj",
    )(oa, ob, sz, sgm, x, w_a.astype(BF16), w_b.astype(BF16), w_o.astype(BF16), norm_post.reshape(1, d))


def _layer(x, norm_pre, w_in, pe_k, w1_k, w2_k, pe_v, w1_v, w2_v, w_a, w_b, w_o, norm_post):
    b, s, d = x.shape
    assert s % TILE == 0 and TILE == MOBA_BLOCK
    qa, qb, vt, sz, gt, kcv, kaug, kbar, sgm = _project(x, norm_pre, w_in)
    kc, vc = _compress(kcv, pe_k, w1_k, w2_k, pe_v, w1_v, w2_v)
    ocmp, selb = _nsa_cmp(qa, kc, vc, gt)
    oa = _nsa_attn(qa, kaug, vt, selb, gt, ocmp)
    nt = s // TILE
    kbar = kbar.reshape(b, nt, MOBA_HEADS, HEAD_DIM).transpose(0, 2, 1, 3)
    nb = max(16, nt)
    kbar = jnp.pad(kbar, ((0, 0), (0, 0), (0, nb - nt), (0, 0)))
    ob = _moba(qb, kaug, vt, kbar)
    oa = oa.reshape(b, NSA_HEADS * HEAD_DIM, s)
    ob = ob.reshape(b, MOBA_HEADS * HEAD_DIM, s)
    return _output(oa, ob, sz, sgm, x, w_a, w_b, w_o, norm_post)


def kernel(x, norm_pre, w_in, cmp_pe_k, cmp_w1_k, cmp_w2_k, cmp_pe_v, cmp_w1_v, cmp_w2_v,
           w_branch_a, w_branch_b, w_o, norm_post):
    h = x
    for l in range(norm_pre.shape[0]):
        h = _layer(h, norm_pre[l], w_in[l], cmp_pe_k[l], cmp_w1_k[l], cmp_w2_k[l],
                   cmp_pe_v[l], cmp_w1_v[l], cmp_w2_v[l], w_branch_a[l], w_branch_b[l],
                   w_o[l], norm_post[l])
    return h
```

```python
import functools

import numpy as np
import jax
import jax.numpy as jnp
from jax import lax
from jax.experimental import pallas as pl
from jax.experimental.pallas import tpu as pltpu

F32 = jnp.float32
BF16 = jnp.bfloat16

HEAD_DIM = 64
NSA_HEADS = 8
NSA_GROUPS = 2
NSA_HPG = NSA_HEADS // NSA_GROUPS
CMP_LEN = 32
CMP_STRIDE = 16
CMP_HIDDEN = 2 * HEAD_DIM
SLC_BLOCK = 64
SLC_TOPN = 16
WINDOW = 512
MOBA_HEADS = 8
MOBA_BLOCK = 256
MOBA_TOPK = 3
TOTAL_HEADS = NSA_HEADS + MOBA_HEADS
RMS_EPS = 1e-6

TILE = 256
V_ROWS = 80
Q_ROWS = 256
K_COLS = 128
NEG = -1e30
LOG2E = float(np.log2(np.e))
QSCALE = float(HEAD_DIM ** -0.5) * LOG2E
VMEM_LIMIT = 56 * 1024 * 1024


def _split3(v):
    v = np.asarray(v, np.float32)
    rnd = lambda a: a.astype(BF16).astype(np.float32)
    hi = rnd(v)
    mid = rnd(v - hi)
    lo = rnd(v - hi - mid)
    return hi, mid, lo


def _alibi_slopes():
    s = (2.0 ** (-8.0 * np.arange(1, TOTAL_HEADS + 1) / TOTAL_HEADS)).astype(np.float32)
    return np.concatenate([s[0::2], s[1::2]])


def _q_aug_rows(width):
    sl = (_alibi_slopes().astype(np.float64) * LOG2E).astype(np.float32)
    hi, mid, lo = _split3(sl)
    a = np.zeros((TOTAL_HEADS, HEAD_DIM, width), np.float32)
    for r, part in enumerate((hi, hi, mid, mid, lo, lo)):
        a[:, r, :] = part[:, None]
    return jnp.asarray(a, BF16)


def _pos_aug_cols(pos):
    pos = np.asarray(pos, np.int64)
    p_hi = (pos // 64 * 64).astype(np.float32)
    p_lo = (pos % 64).astype(np.float32)
    a = np.zeros((pos.shape[0], HEAD_DIM), np.float32)
    for c in range(3):
        a[:, 2 * c] = p_hi
        a[:, 2 * c + 1] = p_lo
    return jnp.asarray(a)


def _sigmoid(x):
    return 1.0 / (1.0 + jnp.exp(-x))


def _hi_lo(x):
    hi = x.astype(BF16)
    lo = (x - hi.astype(F32)).astype(BF16)
    return hi, lo


def _dot(a, b):
    return jnp.dot(a, b, preferred_element_type=F32)


def _dot_nt(a, b):
    return lax.dot_general(a, b, (((1,), (1,)), ((), ())), preferred_element_type=F32)


def _dot3(a, b):
    a_hi, a_lo = _hi_lo(a)
    b_hi, b_lo = _hi_lo(b)
    return _dot(a_hi, b_hi) + _dot(a_hi, b_lo) + _dot(a_lo, b_hi)


N_VT = 12
N_KA = 12
CT_Q = 0
CT_V = 1024
CT_Z = CT_V + N_VT * HEAD_DIM
CT_G = CT_Z + 1024
CT_ROWS = CT_G + 32
CN_KCV = 0
CN_KA = 256
CN_MERGE = CN_KA + N_KA * HEAD_DIM


def _proj_kernel(x_ref, g_ref, wt_ref, wn_ref, qaug_ref, pos_ref,
                 qa_ref, qb_ref, vt_ref, sz_ref, gt_ref, kcv_ref, kaug_ref, kbar_ref, sgm_ref):
    x = x_ref[0]
    tr = x.shape[0]
    ms = jnp.mean(x * x, axis=-1, keepdims=True)
    u = (x * lax.rsqrt(ms + RMS_EPS) * g_ref[...]).astype(BF16)

    acc_t = _dot_nt(wt_ref[...], u)
    for h in range(TOTAL_HEADS):
        q = acc_t[h * HEAD_DIM:(h + 1) * HEAD_DIM] * QSCALE
        hi, lo = _hi_lo(q)
        ref = qa_ref if h < NSA_HEADS else qb_ref
        hh = h % NSA_HEADS
        ref[0, hh, 0:64, :] = hi
        ref[0, hh, 64:128, :] = qaug_ref[h]
        ref[0, hh, 128:192, :] = lo
        ref[0, hh, 192:256, :] = hi
    ones_rows = (lax.broadcasted_iota(jnp.int32, (V_ROWS - HEAD_DIM, tr), 0) == 0).astype(F32)
    for j in range(N_VT):
        v = acc_t[CT_V + j * HEAD_DIM:CT_V + (j + 1) * HEAD_DIM]
        vt_ref[0, j, 0] = jnp.concatenate([v, ones_rows], axis=0).astype(BF16)
    z = acc_t[CT_Z:CT_G]
    sz_ref[0] = (z * _sigmoid(z)).astype(BF16)
    gt_ref[0] = _sigmoid(acc_t[CT_G:CT_ROWS])

    acc_n = _dot(u, wn_ref[...])
    for j in range(4):
        kcv_ref[0, j] = acc_n[:, CN_KCV + j * HEAD_DIM:CN_KCV + (j + 1) * HEAD_DIM]
    pos = pos_ref[...]
    for j in range(N_KA):
        k = acc_n[:, CN_KA + j * HEAD_DIM:CN_KA + (j + 1) * HEAD_DIM]
        kaug_ref[0, j] = jnp.concatenate([k, pos], axis=1).astype(BF16)
    kb = acc_n[:, CN_KA + 4 * HEAD_DIM:CN_MERGE]
    kbar_ref[0, 0] = jnp.mean(kb, axis=0, keepdims=True)
    sgm_ref[0] = _sigmoid(acc_n[:, CN_MERGE:]).astype(BF16)


def _proj_weights(w_in, d_model):
    o_q = 0
    o_kv = o_q + NSA_HEADS * HEAD_DIM
    o_g = o_kv + 6 * NSA_GROUPS * HEAD_DIM
    o_za = o_g + 3 * NSA_HEADS
    o_b = o_za + NSA_HEADS * HEAD_DIM
    o_zb = o_b + 3 * MOBA_HEADS * HEAD_DIM
    o_m = o_zb + MOBA_HEADS * HEAD_DIM
    kv = lambda j: w_in[:, o_kv + j * 128:o_kv + (j + 1) * 128]
    qb = w_in[:, o_b:o_b + 512]
    kb = w_in[:, o_b + 512:o_b + 1024]
    vb = w_in[:, o_b + 1024:o_b + 1536]
    wt = jnp.concatenate([
        w_in[:, o_q:o_kv], qb, kv(3), kv(5), vb,
        w_in[:, o_za:o_b], w_in[:, o_zb:o_m], w_in[:, o_g:o_za],
        jnp.zeros((d_model, CT_ROWS - CT_G - 3 * NSA_HEADS), w_in.dtype)], axis=1)
    wn = jnp.concatenate([kv(0), kv(1), kv(2), kv(4), kb, w_in[:, o_m:]], axis=1)
    return wt.T.astype(BF16), wn.astype(BF16)


def _project(x, norm_pre, w_in):
    b, s, d = x.shape
    nt = s // TILE
    wt, wn = _proj_weights(w_in, d)
    cn = wn.shape[1]
    qaug = _q_aug_rows(TILE)
    pos = _pos_aug_cols(np.arange(s))
    const = lambda *shape: pl.BlockSpec(shape, lambda bi, r: (0,) * len(shape))
    out_shape = (
        jax.ShapeDtypeStruct((b, NSA_HEADS, Q_ROWS, s), BF16),
        jax.ShapeDtypeStruct((b, MOBA_HEADS, Q_ROWS, s), BF16),
        jax.ShapeDtypeStruct((b, N_VT, nt, V_ROWS, TILE), BF16),
        jax.ShapeDtypeStruct((b, 1024, s), BF16),
        jax.ShapeDtypeStruct((b, 32, s), F32),
        jax.ShapeDtypeStruct((b, 4, s, HEAD_DIM), F32),
        jax.ShapeDtypeStruct((b, N_KA, s, K_COLS), BF16),
        jax.ShapeDtypeStruct((b, nt, 1, MOBA_HEADS * HEAD_DIM), F32),
        jax.ShapeDtypeStruct((b, s, 2 * d), BF16),
    )
    out_specs = (
        pl.BlockSpec((1, NSA_HEADS, Q_ROWS, TILE), lambda bi, r: (bi, 0, 0, r)),
        pl.BlockSpec((1, MOBA_HEADS, Q_ROWS, TILE), lambda bi, r: (bi, 0, 0, r)),
        pl.BlockSpec((1, N_VT, 1, V_ROWS, TILE), lambda bi, r: (bi, 0, r, 0, 0)),
        pl.BlockSpec((1, 1024, TILE), lambda bi, r: (bi, 0, r)),
        pl.BlockSpec((1, 32, TILE), lambda bi, r: (bi, 0, r)),
        pl.BlockSpec((1, 4, TILE, HEAD_DIM), lambda bi, r: (bi, 0, r, 0)),
        pl.BlockSpec((1, N_KA, TILE, K_COLS), lambda bi, r: (bi, 0, r, 0)),
        pl.BlockSpec((1, 1, 1, MOBA_HEADS * HEAD_DIM), lambda bi, r: (bi, r, 0, 0)),
        pl.BlockSpec((1, TILE, 2 * d), lambda bi, r: (bi, r, 0)),
    )
    return pl.pallas_call(
        _proj_kernel,
        grid=(b, nt),
        in_specs=[
            pl.BlockSpec((1, TILE, d), lambda bi, r: (bi, r, 0)),
            const(1, d),
            const(CT_ROWS, d),
            const(d, cn),
            const(TOTAL_HEADS, HEAD_DIM, TILE),
            pl.BlockSpec((TILE, HEAD_DIM), lambda bi, r: (r, 0)),
        ],
        out_specs=out_specs,
        out_shape=out_shape,
        compiler_params=pltpu.CompilerParams(vmem_limit_bytes=VMEM_LIMIT),
        name="proj",
    )(x, norm_pre.reshape(1, d), wt, wn, qaug, pos)


def _compress_core(r_ref, pe_ref, w1_ref, w2_ref):
    r = r_ref[0, 0]
    nr = r.shape[0]
    a = _dot3(r + pe_ref[0], w1_ref[0])
    bm = _dot3(r + pe_ref[1], w1_ref[1])
    h = a + pltpu.roll(bm, nr - 1, axis=0)
    h = h * _sigmoid(h)
    c = _dot3(h, w2_ref[...])
    valid = lax.broadcasted_iota(jnp.int32, c.shape, 0) < nr - 1
    return jnp.where(valid, c, 0.0)


def _compress_k_kernel(r_ref, pe_ref, w1_ref, w2_ref, pos_ref, o_ref):
    c = _compress_core(r_ref, pe_ref, w1_ref, w2_ref)
    hi, lo = _hi_lo(c)
    hi = hi.astype(F32)
    o_ref[0, 0] = jnp.concatenate([hi, pos_ref[...], hi, lo.astype(F32)], axis=1).astype(BF16)


def _compress_v_kernel(r_ref, pe_ref, w1_ref, w2_ref, o_ref):
    c = _compress_core(r_ref, pe_ref, w1_ref, w2_ref)
    o_ref[0, 0] = c.T.astype(BF16)


def _compress(kcv, pe_k, w1_k, w2_k, pe_v, w1_v, w2_v):
    b, _, s, hd = kcv.shape
    nr = s // CMP_STRIDE
    half = CMP_STRIDE * hd
    r = kcv.reshape(b, 4, nr, half)
    prep = lambda pe, w1: (pe.reshape(2, 1, half), w1.reshape(2, half, CMP_HIDDEN))
    pos = _pos_aug_cols(np.arange(nr) * CMP_STRIDE + CMP_LEN - 1)
    const = lambda *shape: pl.BlockSpec(shape, lambda bi, g: (0,) * len(shape))
    common = [const(2, 1, half), const(2, half, CMP_HIDDEN), const(CMP_HIDDEN, hd)]
    pe, w1 = prep(pe_k, w1_k)
    kc = pl.pallas_call(
        _compress_k_kernel,
        grid=(b, NSA_GROUPS),
        in_specs=[pl.BlockSpec((1, 1, nr, half), lambda bi, g: (bi, g, 0, 0))] + common + [const(nr, hd)],
        out_specs=pl.BlockSpec((1, 1, nr, 4 * hd), lambda bi, g: (bi, g, 0, 0)),
        out_shape=jax.ShapeDtypeStruct((b, NSA_GROUPS, nr, 4 * hd), BF16),
        name="compress_k",
    )(r, pe, w1, w2_k, pos)
    pe, w1 = prep(pe_v, w1_v)
    vc = pl.pallas_call(
        _compress_v_kernel,
        grid=(b, NSA_GROUPS),
        in_specs=[pl.BlockSpec((1, 1, nr, half), lambda bi, g: (bi, g + NSA_GROUPS, 0, 0))] + common,
        out_specs=pl.BlockSpec((1, 1, hd, nr), lambda bi, g: (bi, g, 0, 0)),
        out_shape=jax.ShapeDtypeStruct((b, NSA_GROUPS, hd, nr), BF16),
        name="compress_v",
    )(r, pe, w1, w2_v)
    return kc, vc


def _rank_select(v, n_rows, topn):
    sub = 8
    ng = n_rows // sub
    groups = [v[g * sub:(g + 1) * sub, :] for g in range(ng)]
    cnt = [jnp.zeros(groups[0].shape, F32) for _ in range(ng)]
    jloc = lax.broadcasted_iota(jnp.int32, groups[0].shape, 0)
    for i in range(n_rows):
        vi = v[i:i + 1, :]
        gi = i // sub
        for g in range(ng):
            if g < gi:
                one = jnp.where(vi > groups[g], 1.0, 0.0)
            elif g > gi:
                one = jnp.where(vi >= groups[g], 1.0, 0.0)
            else:
                ge = jnp.where(vi >= groups[g], 1.0, 0.0)
                gt = jnp.where(vi > groups[g], 1.0, 0.0)
                one = jnp.where(jloc > i - gi * sub, ge, gt)
            cnt[g] = cnt[g] + one
    return jnp.concatenate(cnt, axis=0) < float(topn)


def _nsa_cmp_kernel(q_ref, kc_ref, vc_ref, g_ref, ov_ref, ocmp_ref, selb_ref, s_ref, m_ref, v_ref):
    i = pl.program_id(2)
    g = pl.program_id(1)
    kc = kc_ref[0, 0]
    vc = vc_ref[0, 0]
    nc = kc.shape[0]
    tq = q_ref.shape[-1]
    n_idx = lax.broadcasted_iota(jnp.int32, (nc, tq), 0)
    t = lax.broadcasted_iota(jnp.int32, (nc, tq), 1) + i * tq
    bias = jnp.where(t >= n_idx * CMP_STRIDE + (CMP_LEN - 1), 0.0, NEG)
    any_valid = lax.broadcasted_iota(jnp.int32, (1, tq), 1) + i * tq >= CMP_LEN - 1
    for p in range(NSA_HPG):
        s = _dot(kc, q_ref[0, p]) + bias
        s_ref[p] = s
        m_ref[p] = jnp.max(s, axis=0, keepdims=True)
    psum = jnp.zeros((nc, tq), F32)
    for p in range(NSA_HPG):
        e = jnp.exp2(s_ref[p] - m_ref[p])
        l = jnp.sum(e, axis=0, keepdims=True)
        rinv = jnp.where(any_valid, 1.0 / jnp.maximum(l, 1e-30), 0.0)
        gate = g_ref[0, pl.ds(g * NSA_HPG + p, 1), :]
        ocmp_ref[0, p] = _dot(vc, e.astype(BF16)) * (rinv * gate)
        psum = psum + e * rinv
    p_hi, p_lo = _hi_lo(psum)
    imp = _dot(ov_ref[...], p_hi) + _dot(ov_ref[...], p_lo)
    n_slc = imp.shape[0]
    blk = lax.broadcasted_iota(jnp.int32, (n_slc, tq), 0)
    cur = (lax.broadcasted_iota(jnp.int32, (n_slc, tq), 1) + i * tq) // SLC_BLOCK
    forced = (blk == 0) | (blk == cur) | (blk == cur - 1)
    v = jnp.where(forced, 3e38, imp)
    v_ref[...] = jnp.where(blk <= cur, v, NEG)

    @pl.when(i >= 0)
    def _():
        v = v_ref[...]
        sel = _rank_select(v, n_slc, SLC_TOPN)
        selb_ref[0, 0] = jnp.where(sel & (v > 0.5 * NEG), 0.0, NEG)


def _nsa_cmp(qa, kc, vc, gt):
    b, _, _, s = qa.shape
    nq = s // TILE
    nc = kc.shape[2]
    n_slc = s // SLC_BLOCK
    cs = np.arange(nc) * CMP_STRIDE
    ss = np.arange(n_slc) * SLC_BLOCK
    ov = ((cs[None, :] <= ss[:, None] + SLC_BLOCK - 1) & (cs[None, :] + CMP_LEN - 1 >= ss[:, None]))
    ov[:, nc - 1] = False
    ov = jnp.asarray(ov.astype(np.float32), BF16)
    return pl.pallas_call(
        _nsa_cmp_kernel,
        grid=(b, NSA_GROUPS, nq),
        in_specs=[
            pl.BlockSpec((1, NSA_HPG, Q_ROWS, TILE), lambda bi, g, i: (bi, g, 0, i)),
            pl.BlockSpec((1, 1, nc, 4 * HEAD_DIM), lambda bi, g, i: (bi, g, 0, 0)),
            pl.BlockSpec((1, 1, HEAD_DIM, nc), lambda bi, g, i: (bi, g, 0, 0)),
            pl.BlockSpec((1, 32, TILE), lambda bi, g, i: (bi, 0, i)),
            pl.BlockSpec((n_slc, nc), lambda bi, g, i: (0, 0)),
        ],
        out_specs=(
            pl.BlockSpec((1, NSA_HPG, HEAD_DIM, TILE), lambda bi, g, i: (bi, g, 0, i)),
            pl.BlockSpec((1, 1, n_slc, TILE), lambda bi, g, i: (bi, g, 0, i)),
        ),
        out_shape=(
            jax.ShapeDtypeStruct((b, NSA_HEADS, HEAD_DIM, s), F32),
            jax.ShapeDtypeStruct((b, NSA_GROUPS, n_slc, s), F32),
        ),
        scratch_shapes=[pltpu.VMEM((NSA_HPG, nc, TILE), F32),
                        pltpu.VMEM((NSA_HPG, 1, TILE), F32),
                        pltpu.VMEM((n_slc, TILE), F32)],
        name="nsa_cmp",
    )(qa, kc, vc, gt, ov)


def _score_step(k_rows, q, bias, s_ref, h, slot, m_ref, midx, first):
    n = k_rows.shape[0] // TILE
    s = _dot(k_rows, q)
    if bias is not None:
        s = s + bias
    s_ref[h, pl.ds(slot, n)] = s.reshape(n, TILE, s.shape[-1])
    mt = jnp.max(s, axis=0, keepdims=True)
    m_ref[midx] = mt if first else jnp.maximum(m_ref[midx], mt)


def _value_step(v_tiles, s_ref, h, slot, m_ref, midx, acc_ref, aidx, first):
    n = len(v_tiles)
    p = jnp.exp2(s_ref[h, pl.ds(slot, n)] - m_ref[midx]).astype(BF16)
    pv = _dot(v_tiles[0], p[0])
    for j in range(1, n):
        pv = pv + _dot(v_tiles[j], p[j])
    acc_ref[aidx] = pv if first else acc_ref[aidx] + pv


def _rows2(row0, row1, tq):
    return jnp.concatenate([jnp.broadcast_to(row0, (TILE, tq)), jnp.broadcast_to(row1, (TILE, tq))], axis=0)


def _finish(acc_ref, idx):
    acc = acc_ref[idx]
    return acc[0:HEAD_DIM] / acc[HEAD_DIM:HEAD_DIM + 1]


def _tri_biases(tk, tq):
    r = lax.broadcasted_iota(jnp.int32, (tk, tq), 0)
    c = lax.broadcasted_iota(jnp.int32, (tk, tq), 1)
    causal = jnp.where(r > c, NEG, 0.0)
    far = jnp.where(r > c, 0.0, NEG)
    return causal, far


def _nsa_attn_kernel(q_ref, ksl_ref, vsl_ref, kwn_ref, vwn_ref, selb_ref, g_ref, ocmp_ref,
                     o_ref, s0_ref, sw0_ref, m0_ref, s1_ref, sw1_ref, m1_ref, acc_ref):
    i = pl.program_id(2)
    g = pl.program_id(1)
    tq = q_ref.shape[-1]
    nt = s0_ref.shape[1] - 1
    causal, far = _tri_biases(TILE, tq)
    nblk = TILE // SLC_BLOCK
    heads = range(NSA_HPG)
    w = NSA_HPG
    q = lambda p: q_ref[0, p, 0:K_COLS, :]
    banks = ((s0_ref, sw0_ref, m0_ref), (s1_ref, sw1_ref, m1_ref))

    def sel_bias(kt, valid):
        rows = []
        for j in range(nblk):
            row = selb_ref[0, 0, pl.ds(kt * nblk + j, 1), :]
            if valid is not None:
                row = jnp.where(valid, row, NEG)
            rows.append(jnp.broadcast_to(row, (SLC_BLOCK, tq)))
        return jnp.concatenate(rows, axis=0)

    def k_rows(ref, start, n):
        return ref[0, 0, pl.ds(pl.multiple_of(start * TILE, TILE), n * TILE), :]

    def scores_head(t, b):
        sb, swb, mb = b
        bias = sel_bias(t, None) + causal
        k_diag = k_rows(ksl_ref, t, 1)
        t1 = jnp.maximum(t - 1, 0)
        t2 = jnp.maximum(t - 2, 0)
        bias1 = jnp.where(t >= 1, 0.0, NEG)
        bias2 = far + jnp.where(t >= 2, 0.0, NEG)
        kw0, kw1, kw2 = k_rows(kwn_ref, t, 1), k_rows(kwn_ref, t1, 1), k_rows(kwn_ref, t2, 1)
        for p in heads:
            _score_step(k_diag, q(p), bias, sb, p, nt, mb, p, True)
            _score_step(kw0, q(p), causal, swb, p, 0, mb, w + p, True)
            _score_step(kw1, q(p), bias1, swb, p, 1, mb, w + p, False)
            _score_step(kw2, q(p), bias2, swb, p, 2, mb, w + p, False)

    def scores_pair(t, c, b):
        bias = jnp.concatenate([sel_bias(2 * c, None), sel_bias(2 * c + 1, 2 * c + 1 < t)], axis=0)
        k2 = k_rows(ksl_ref, 2 * c, 2)
        for p in heads:
            _score_step(k2, q(p), bias, b[0], p, 2 * c, b[2], p, False)

    def values_head(t, b):
        sb, swb, mb = b
        t1 = jnp.maximum(t - 1, 0)
        t2 = jnp.maximum(t - 2, 0)
        v_diag = [vsl_ref[0, 0, t]]
        v_win = [vwn_ref[0, 0, t], vwn_ref[0, 0, t1], vwn_ref[0, 0, t2]]
        for p in heads:
            _value_step(v_diag, sb, p, nt, mb, p, acc_ref, p, True)
            _value_step(v_win, swb, p, 0, mb, w + p, acc_ref, w + p, True)

    def values_pair(c, b):
        v2 = [vsl_ref[0, 0, 2 * c], vsl_ref[0, 0, 2 * c + 1]]
        for p in heads:
            _value_step(v2, b[0], p, 2 * c, b[2], p, acc_ref, p, False)

    def finish():
        for p in heads:
            row = g * NSA_HPG + p
            g_slc = g_ref[0, pl.ds(NSA_HEADS + row, 1), :]
            g_win = g_ref[0, pl.ds(2 * NSA_HEADS + row, 1), :]
            o_ref[0, p] = ocmp_ref[0, p] + g_slc * _finish(acc_ref, p) + g_win * _finish(acc_ref, w + p)

    @pl.when(i == 0)
    def _():
        scores_head(i, banks[0])

    def steady(cur, prv, odd):
        scores_head(i, cur)
        values_head(i - 1, prv)

        def body(c, carry):
            scores_pair(i, c, cur)
            values_pair(c, prv)
            return carry
        lax.fori_loop(0, i // 2, body, 0)
        if odd:
            scores_pair(i, i // 2, cur)
        finish()

    for par in (0, 1):
        @pl.when((i >= 1) & (i < nt) & (lax.rem(i, 2) == par))
        def _():
            steady(banks[par], banks[1 - par], par == 1)

    @pl.when(i == nt)
    def _():
        values_head(i - 1, banks[1])

        def body(c, carry):
            values_pair(c, banks[1])
            return carry
        lax.fori_loop(0, i // 2, body, 0)
        finish()


def _nsa_attn(qa, kaug, vt, selb, gt, ocmp):
    b, _, _, s = qa.shape
    nq = s // TILE
    n_slc = s // SLC_BLOCK
    assert WINDOW == 2 * TILE and TILE % SLC_BLOCK == 0 and nq % 2 == 0
    kspec = lambda off: pl.BlockSpec((1, 1, s, K_COLS), lambda bi, g, i: (bi, g + off, 0, 0))
    vspec = lambda off: pl.BlockSpec((1, 1, nq, V_ROWS, TILE), lambda bi, g, i: (bi, g + off, 0, 0, 0))
    p1 = lambda i: jnp.minimum(i, nq - 1)
    p2 = lambda i: jnp.maximum(i - 1, 0)
    return pl.pallas_call(
        _nsa_attn_kernel,
        grid=(b, NSA_GROUPS, nq + 1),
        in_specs=[
            pl.BlockSpec((1, NSA_HPG, Q_ROWS, TILE), lambda bi, g, i: (bi, g, 0, p1(i))),
            kspec(0), vspec(0), kspec(NSA_GROUPS), vspec(NSA_GROUPS),
            pl.BlockSpec((1, 1, n_slc, TILE), lambda bi, g, i: (bi, g, 0, p1(i))),
            pl.BlockSpec((1, 32, TILE), lambda bi, g, i: (bi, 0, p2(i))),
            pl.BlockSpec((1, NSA_HPG, HEAD_DIM, TILE), lambda bi, g, i: (bi, g, 0, p2(i))),
        ],
        out_specs=pl.BlockSpec((1, NSA_HPG, HEAD_DIM, TILE), lambda bi, g, i: (bi, g, 0, p2(i))),
        out_shape=jax.ShapeDtypeStruct((b, NSA_HEADS, HEAD_DIM, s), F32),
        scratch_shapes=2 * [pltpu.VMEM((NSA_HPG, nq + 1, TILE, TILE), F32),
                            pltpu.VMEM((NSA_HPG, 3, TILE, TILE), F32),
                            pltpu.VMEM((2 * NSA_HPG, 1, TILE), F32)]
                       + [pltpu.VMEM((2 * NSA_HPG, V_ROWS, TILE), F32)],
        compiler_params=pltpu.CompilerParams(vmem_limit_bytes=VMEM_LIMIT),
        name="nsa_attn",
    )(qa, kaug, vt, kaug, vt, selb, gt, ocmp)


MOBA_HPS = 4


def _moba_kernel(q_ref, k_ref, v_ref, kbar_ref, o_ref, s0_ref, m0_ref, s1_ref, m1_ref, acc_ref, sb_ref):
    i = pl.program_id(2)
    tq = q_ref.shape[-1]
    nt = s0_ref.shape[1] - 1
    causal, _ = _tri_biases(TILE, tq)
    heads = range(MOBA_HPS)
    q = lambda h: q_ref[0, h, 0:K_COLS, :]
    banks = ((s0_ref, m0_ref), (s1_ref, m1_ref))

    def k_rows(h, start, n):
        return k_ref[0, h, pl.ds(pl.multiple_of(start * TILE, TILE), n * TILE), :]

    def scores_head(t, b):
        for h in heads:
            kb_hi, kb_lo = _hi_lo(kbar_ref[0, h])
            gate = (_dot(kb_hi, q_ref[0, h, 0:64, :]) + _dot(kb_hi, q_ref[0, h, 128:192, :])
                    + _dot(kb_lo, q_ref[0, h, 192:256, :]))
            nb = gate.shape[0]
            past = lax.broadcasted_iota(jnp.int32, (nb, tq), 0) < t
            sel = _rank_select(jnp.where(past, gate, NEG), nb, MOBA_TOPK)
            sb_ref[h] = jnp.where(sel & past, 0.0, NEG)
            _score_step(k_rows(h, t, 1), q(h), causal, b[0], h, nt, b[1], h, True)

    def scores_pair(c, b):
        for h in heads:
            bias = _rows2(sb_ref[h, pl.ds(2 * c, 1), :], sb_ref[h, pl.ds(2 * c + 1, 1), :], tq)
            _score_step(k_rows(h, 2 * c, 2), q(h), bias, b[0], h, 2 * c, b[1], h, False)

    def values_head(t, b):
        for h in heads:
            _value_step([v_ref[0, h, t]], b[0], h, nt, b[1], h, acc_ref, h, True)

    def values_pair(c, b):
        for h in heads:
            _value_step([v_ref[0, h, 2 * c], v_ref[0, h, 2 * c + 1]], b[0], h, 2 * c, b[1], h, acc_ref, h, False)

    def finish():
        for h in heads:
            o_ref[0, h] = _finish(acc_ref, h)

    @pl.when(i == 0)
    def _():
        scores_head(i, banks[0])

    def steady(cur, prv, odd):
        scores_head(i, cur)
        values_head(i - 1, prv)

        def body(c, carry):
            scores_pair(c, cur)
            values_pair(c, prv)
            return carry
        lax.fori_loop(0, i // 2, body, 0)
        if odd:
            scores_pair(i // 2, cur)
        finish()

    for par in (0, 1):
        @pl.when((i >= 1) & (i < nt) & (lax.rem(i, 2) == par))
        def _():
            steady(banks[par], banks[1 - par], par == 1)

    @pl.when(i == nt)
    def _():
        values_head(i - 1, banks[1])

        def body(c, carry):
            values_pair(c, banks[1])
            return carry
        lax.fori_loop(0, i // 2, body, 0)
        finish()


def _moba(qb, kaug, vt, kbar):
    b, _, _, s = qb.shape
    nq = s // TILE
    nb = kbar.shape[2]
    assert nq % 2 == 0 and nb >= nq
    hps = MOBA_HPS
    off = 2 * NSA_GROUPS // hps
    p1 = lambda i: jnp.minimum(i, nq - 1)
    p2 = lambda i: jnp.maximum(i - 1, 0)
    return pl.pallas_call(
        _moba_kernel,
        grid=(b, MOBA_HEADS // hps, nq + 1),
        in_specs=[
            pl.BlockSpec((1, hps, Q_ROWS, TILE), lambda bi, h, i: (bi, h, 0, p1(i))),
            pl.BlockSpec((1, hps, s, K_COLS), lambda bi, h, i: (bi, h + off, 0, 0)),
            pl.BlockSpec((1, hps, nq, V_ROWS, TILE), lambda bi, h, i: (bi, h + off, 0, 0, 0)),
            pl.BlockSpec((1, hps, nb, HEAD_DIM), lambda bi, h, i: (bi, h, 0, 0)),
        ],
        out_specs=pl.BlockSpec((1, hps, HEAD_DIM, TILE), lambda bi, h, i: (bi, h, 0, p2(i))),
        out_shape=jax.ShapeDtypeStruct((b, MOBA_HEADS, HEAD_DIM, s), F32),
        scratch_shapes=2 * [pltpu.VMEM((hps, nq + 1, TILE, TILE), F32),
                            pltpu.VMEM((hps, 1, TILE), F32)]
                       + [pltpu.VMEM((hps, V_ROWS, TILE), F32),
                          pltpu.VMEM((hps, nb, TILE), F32)],
        compiler_params=pltpu.CompilerParams(vmem_limit_bytes=VMEM_LIMIT),
        name="moba",
    )(qb, kaug, vt, kbar)


def _out_kernel(oa_ref, ob_ref, sz_ref, sgm_ref, x_ref, wa_ref, wb_ref, wo_ref, g_ref, o_ref):
    d = x_ref.shape[-1]
    w = NSA_HEADS * HEAD_DIM
    ya = (oa_ref[0] * sz_ref[0, 0:w, :].astype(F32)).T.astype(BF16)
    yb = (ob_ref[0] * sz_ref[0, w:, :].astype(F32)).T.astype(BF16)
    sg = sgm_ref[0].astype(F32)
    m = sg[:, 0:d] * _dot(ya, wa_ref[...]) + sg[:, d:] * _dot(yb, wb_ref[...])
    r = _dot(m.astype(BF16), wo_ref[...])
    ms = jnp.mean(r * r, axis=-1, keepdims=True)
    o_ref[0] = x_ref[0] + r * lax.rsqrt(ms + RMS_EPS) * g_ref[...]


def _output(oa, ob, sz, sgm, x, w_a, w_b, w_o, norm_post):
    b, s, d = x.shape
    nt = s // TILE
    w = oa.shape[1]
    const = lambda *shape: pl.BlockSpec(shape, lambda bi, r: (0,) * len(shape))
    return pl.pallas_call(
        _out_kernel,
        grid=(b, nt),
        in_specs=[
            pl.BlockSpec((1, w, TILE), lambda bi, r: (bi, 0, r)),
            pl.BlockSpec((1, w, TILE), lambda bi, r: (bi, 0, r)),
            pl.BlockSpec((1, 2 * w, TILE), lambda bi, r: (bi, 0, r)),
            pl.BlockSpec((1, TILE, 2 * d), lambda bi, r: (bi, r, 0)),
            pl.BlockSpec((1, TILE, d), lambda bi, r: (bi, r, 0)),
            const(w, d), const(w, d), const(d, d), const(1, d),
        ],
        out_specs=pl.BlockSpec((1, TILE, d), lambda bi, r: (bi, r, 0)),
        out_shape=jax.ShapeDtypeStruct((b, s, d), F32),
        compiler_params=pltpu.CompilerParams(vmem_limit_bytes=VMEM_LIMIT),
        name="out_proj",
    )(oa, ob, sz, sgm, x, w_a.astype(BF16), w_b.astype(BF16), w_o.astype(BF16), norm_post.reshape(1, d))


def _layer(x, norm_pre, w_in, pe_k, w1_k, w2_k, pe_v, w1_v, w2_v, w_a, w_b, w_o, norm_post):
    b, s, d = x.shape
    assert s % TILE == 0 and TILE == MOBA_BLOCK
    qa, qb, vt, sz, gt, kcv, kaug, kbar, sgm = _project(x, norm_pre, w_in)
    kc, vc = _compress(kcv, pe_k, w1_k, w2_k, pe_v, w1_v, w2_v)
    ocmp, selb = _nsa_cmp(qa, kc, vc, gt)
    oa = _nsa_attn(qa, kaug, vt, selb, gt, ocmp)
    nt = s // TILE
    kbar = kbar.reshape(b, nt, MOBA_HEADS, HEAD_DIM).transpose(0, 2, 1, 3)
    nb = max(16, nt)
    kbar = jnp.pad(kbar, ((0, 0), (0, 0), (0, nb - nt), (0, 0)))
    ob = _moba(qb, kaug, vt, kbar)
    oa = oa.reshape(b, NSA_HEADS * HEAD_DIM, s)
    ob = ob.reshape(b, MOBA_HEADS * HEAD_DIM, s)
    return _output(oa, ob, sz, sgm, x, w_a, w_b, w_o, norm_post)


def kernel(x, norm_pre, w_in, cmp_pe_k, cmp_w1_k, cmp_w2_k, cmp_pe_v, cmp_w1_v, cmp_w2_v,
           w_branch_a, w_branch_b, w_o, norm_post):
    h = x
    for l in range(norm_pre.shape[0]):
        h = _layer(h, norm_pre[l], w_in[l], cmp_pe_k[l], cmp_w1_k[l], cmp_w2_k[l],
                   cmp_pe_v[l], cmp_w1_v[l], cmp_w2_v[l], w_branch_a[l], w_branch_b[l],
                   w_o[l], norm_post[l])
    return h
```

```python
import functools

import numpy as np
import jax
import jax.numpy as jnp
from jax import lax
from jax.experimental import pallas as pl
from jax.experimental.pallas import tpu as pltpu

F32 = jnp.float32
BF16 = jnp.bfloat16

HEAD_DIM = 64
NSA_HEADS = 8
NSA_GROUPS = 2
NSA_HPG = NSA_HEADS // NSA_GROUPS
CMP_LEN = 32
CMP_STRIDE = 16
CMP_HIDDEN = 2 * HEAD_DIM
SLC_BLOCK = 64
SLC_TOPN = 16
WINDOW = 512
MOBA_HEADS = 8
MOBA_BLOCK = 256
MOBA_TOPK = 3
TOTAL_HEADS = NSA_HEADS + MOBA_HEADS
RMS_EPS = 1e-6

TILE = 256
V_ROWS = 80
Q_ROWS = 256
K_COLS = 128
NEG = -1e30
LOG2E = float(np.log2(np.e))
QSCALE = float(HEAD_DIM ** -0.5) * LOG2E
VMEM_LIMIT = 56 * 1024 * 1024


def _split3(v):
    v = np.asarray(v, np.float32)
    rnd = lambda a: a.astype(BF16).astype(np.float32)
    hi = rnd(v)
    mid = rnd(v - hi)
    lo = rnd(v - hi - mid)
    return hi, mid, lo


def _alibi_slopes():
    s = (2.0 ** (-8.0 * np.arange(1, TOTAL_HEADS + 1) / TOTAL_HEADS)).astype(np.float32)
    return np.concatenate([s[0::2], s[1::2]])


def _q_aug_rows(width):
    sl = (_alibi_slopes().astype(np.float64) * LOG2E).astype(np.float32)
    hi, mid, lo = _split3(sl)
    a = np.zeros((TOTAL_HEADS, HEAD_DIM, width), np.float32)
    for r, part in enumerate((hi, hi, mid, mid, lo, lo)):
        a[:, r, :] = part[:, None]
    return jnp.asarray(a, BF16)


def _pos_aug_cols(pos):
    pos = np.asarray(pos, np.int64)
    p_hi = (pos // 64 * 64).astype(np.float32)
    p_lo = (pos % 64).astype(np.float32)
    a = np.zeros((pos.shape[0], HEAD_DIM), np.float32)
    for c in range(3):
        a[:, 2 * c] = p_hi
        a[:, 2 * c + 1] = p_lo
    return jnp.asarray(a)


def _sigmoid(x):
    return 1.0 / (1.0 + jnp.exp(-x))


def _hi_lo(x):
    hi = x.astype(BF16)
    lo = (x - hi.astype(F32)).astype(BF16)
    return hi, lo


def _dot(a, b):
    return jnp.dot(a, b, preferred_element_type=F32)


def _dot_nt(a, b):
    return lax.dot_general(a, b, (((1,), (1,)), ((), ())), preferred_element_type=F32)


def _dot3(a, b):
    a_hi, a_lo = _hi_lo(a)
    b_hi, b_lo = _hi_lo(b)
    return _dot(a_hi, b_hi) + _dot(a_hi, b_lo) + _dot(a_lo, b_hi)


N_VT = 12
N_KA = 12
CT_Q = 0
CT_V = 1024
CT_Z = CT_V + N_VT * HEAD_DIM
CT_G = CT_Z + 1024
CT_ROWS = CT_G + 32
CN_KCV = 0
CN_KA = 256
CN_MERGE = CN_KA + N_KA * HEAD_DIM


def _proj_kernel(x_ref, g_ref, wt_ref, wn_ref, qaug_ref, pos_ref,
                 qa_ref, qb_ref, vt_ref, sz_ref, gt_ref, kcv_ref, kaug_ref, kbar_ref, sgm_ref):
    x = x_ref[0]
    tr = x.shape[0]
    ms = jnp.mean(x * x, axis=-1, keepdims=True)
    u = (x * lax.rsqrt(ms + RMS_EPS) * g_ref[...]).astype(BF16)

    acc_t = _dot_nt(wt_ref[...], u)
    for h in range(TOTAL_HEADS):
        q = acc_t[h * HEAD_DIM:(h + 1) * HEAD_DIM] * QSCALE
        hi, lo = _hi_lo(q)
        ref = qa_ref if h < NSA_HEADS else qb_ref
        hh = h % NSA_HEADS
        ref[0, hh, 0:64, :] = hi
        ref[0, hh, 64:128, :] = qaug_ref[h]
        ref[0, hh, 128:192, :] = lo
        ref[0, hh, 192:256, :] = hi
    ones_rows = (lax.broadcasted_iota(jnp.int32, (V_ROWS - HEAD_DIM, tr), 0) == 0).astype(F32)
    for j in range(N_VT):
        v = acc_t[CT_V + j * HEAD_DIM:CT_V + (j + 1) * HEAD_DIM]
        vt_ref[0, j, 0] = jnp.concatenate([v, ones_rows], axis=0).astype(BF16)
    z = acc_t[CT_Z:CT_G]
    sz_ref[0] = (z * _sigmoid(z)).astype(BF16)
    gt_ref[0] = _sigmoid(acc_t[CT_G:CT_ROWS])

    acc_n = _dot(u, wn_ref[...])
    for j in range(4):
        kcv_ref[0, j] = acc_n[:, CN_KCV + j * HEAD_DIM:CN_KCV + (j + 1) * HEAD_DIM]
    pos = pos_ref[...]
    for j in range(N_KA):
        k = acc_n[:, CN_KA + j * HEAD_DIM:CN_KA + (j + 1) * HEAD_DIM]
        kaug_ref[0, j] = jnp.concatenate([k, pos], axis=1).astype(BF16)
    kb = acc_n[:, CN_KA + 4 * HEAD_DIM:CN_MERGE]
    kbar_ref[0, 0] = jnp.mean(kb, axis=0, keepdims=True)
    sgm_ref[0] = _sigmoid(acc_n[:, CN_MERGE:]).astype(BF16)


def _proj_weights(w_in, d_model):
    o_q = 0
    o_kv = o_q + NSA_HEADS * HEAD_DIM
    o_g = o_kv + 6 * NSA_GROUPS * HEAD_DIM
    o_za = o_g + 3 * NSA_HEADS
    o_b = o_za + NSA_HEADS * HEAD_DIM
    o_zb = o_b + 3 * MOBA_HEADS * HEAD_DIM
    o_m = o_zb + MOBA_HEADS * HEAD_DIM
    kv = lambda j: w_in[:, o_kv + j * 128:o_kv + (j + 1) * 128]
    qb = w_in[:, o_b:o_b + 512]
    kb = w_in[:, o_b + 512:o_b + 1024]
    vb = w_in[:, o_b + 1024:o_b + 1536]
    wt = jnp.concatenate([
        w_in[:, o_q:o_kv], qb, kv(3), kv(5), vb,
        w_in[:, o_za:o_b], w_in[:, o_zb:o_m], w_in[:, o_g:o_za],
        jnp.zeros((d_model, CT_ROWS - CT_G - 3 * NSA_HEADS), w_in.dtype)], axis=1)
    wn = jnp.concatenate([kv(0), kv(1), kv(2), kv(4), kb, w_in[:, o_m:]], axis=1)
    return wt.T.astype(BF16), wn.astype(BF16)


def _project(x, norm_pre, w_in):
    b, s, d = x.shape
    nt = s // TILE
    wt, wn = _proj_weights(w_in, d)
    cn = wn.shape[1]
    qaug = _q_aug_rows(TILE)
    pos = _pos_aug_cols(np.arange(s))
    const = lambda *shape: pl.BlockSpec(shape, lambda bi, r: (0,) * len(shape))
    out_shape = (
        jax.ShapeDtypeStruct((b, NSA_HEADS, Q_ROWS, s), BF16),
        jax.ShapeDtypeStruct((b, MOBA_HEADS, Q_ROWS, s), BF16),
        jax.ShapeDtypeStruct((b, N_VT, nt, V_ROWS, TILE), BF16),
        jax.ShapeDtypeStruct((b, 1024, s), BF16),
        jax.ShapeDtypeStruct((b, 32, s), F32),
        jax.ShapeDtypeStruct((b, 4, s, HEAD_DIM), F32),
        jax.ShapeDtypeStruct((b, N_KA, s, K_COLS), BF16),
        jax.ShapeDtypeStruct((b, nt, 1, MOBA_HEADS * HEAD_DIM), F32),
        jax.ShapeDtypeStruct((b, s, 2 * d), BF16),
    )
    out_specs = (
        pl.BlockSpec((1, NSA_HEADS, Q_ROWS, TILE), lambda bi, r: (bi, 0, 0, r)),
        pl.BlockSpec((1, MOBA_HEADS, Q_ROWS, TILE), lambda bi, r: (bi, 0, 0, r)),
        pl.BlockSpec((1, N_VT, 1, V_ROWS, TILE), lambda bi, r: (bi, 0, r, 0, 0)),
        pl.BlockSpec((1, 1024, TILE), lambda bi, r: (bi, 0, r)),
        pl.BlockSpec((1, 32, TILE), lambda bi, r: (bi, 0, r)),
        pl.BlockSpec((1, 4, TILE, HEAD_DIM), lambda bi, r: (bi, 0, r, 0)),
        pl.BlockSpec((1, N_KA, TILE, K_COLS), lambda bi, r: (bi, 0, r, 0)),
        pl.BlockSpec((1, 1, 1, MOBA_HEADS * HEAD_DIM), lambda bi, r: (bi, r, 0, 0)),
        pl.BlockSpec((1, TILE, 2 * d), lambda bi, r: (bi, r, 0)),
    )
    return pl.pallas_call(
        _proj_kernel,
        grid=(b, nt),
        in_specs=[
            pl.BlockSpec((1, TILE, d), lambda bi, r: (bi, r, 0)),
            const(1, d),
            const(CT_ROWS, d),
            const(d, cn),
            const(TOTAL_HEADS, HEAD_DIM, TILE),
            pl.BlockSpec((TILE, HEAD_DIM), lambda bi, r: (r, 0)),
        ],
        out_specs=out_specs,
        out_shape=out_shape,
        compiler_params=pltpu.CompilerParams(vmem_limit_bytes=VMEM_LIMIT),
        name="proj",
    )(x, norm_pre.reshape(1, d), wt, wn, qaug, pos)


def _compress_core(r_ref, pe_ref, w1_ref, w2_ref):
    r = r_ref[0, 0]
    nr = r.shape[0]
    a = _dot3(r + pe_ref[0], w1_ref[0])
    bm = _dot3(r + pe_ref[1], w1_ref[1])
    h = a + pltpu.roll(bm, nr - 1, axis=0)
    h = h * _sigmoid(h)
    c = _dot3(h, w2_ref[...])
    valid = lax.broadcasted_iota(jnp.int32, c.shape, 0) < nr - 1
    return jnp.where(valid, c, 0.0)


def _compress_k_kernel(r_ref, pe_ref, w1_ref, w2_ref, pos_ref, o_ref):
    c = _compress_core(r_ref, pe_ref, w1_ref, w2_ref)
    hi, lo = _hi_lo(c)
    hi = hi.astype(F32)
    o_ref[0, 0] = jnp.concatenate([hi, pos_ref[...], hi, lo.astype(F32)], axis=1).astype(BF16)


def _compress_v_kernel(r_ref, pe_ref, w1_ref, w2_ref, o_ref):
    c = _compress_core(r_ref, pe_ref, w1_ref, w2_ref)
    o_ref[0, 0] = c.T.astype(BF16)


def _compress(kcv, pe_k, w1_k, w2_k, pe_v, w1_v, w2_v):
    b, _, s, hd = kcv.shape
    nr = s // CMP_STRIDE
    half = CMP_STRIDE * hd
    r = kcv.reshape(b, 4, nr, half)
    prep = lambda pe, w1: (pe.reshape(2, 1, half), w1.reshape(2, half, CMP_HIDDEN))
    pos = _pos_aug_cols(np.arange(nr) * CMP_STRIDE + CMP_LEN - 1)
    const = lambda *shape: pl.BlockSpec(shape, lambda bi, g: (0,) * len(shape))
    common = [const(2, 1, half), const(2, half, CMP_HIDDEN), const(CMP_HIDDEN, hd)]
    pe, w1 = prep(pe_k, w1_k)
    kc = pl.pallas_call(
        _compress_k_kernel,
        grid=(b, NSA_GROUPS),
        in_specs=[pl.BlockSpec((1, 1, nr, half), lambda bi, g: (bi, g, 0, 0))] + common + [const(nr, hd)],
        out_specs=pl.BlockSpec((1, 1, nr, 4 * hd), lambda bi, g: (bi, g, 0, 0)),
        out_shape=jax.ShapeDtypeStruct((b, NSA_GROUPS, nr, 4 * hd), BF16),
        name="compress_k",
    )(r, pe, w1, w2_k, pos)
    pe, w1 = prep(pe_v, w1_v)
    vc = pl.pallas_call(
        _compress_v_kernel,
        grid=(b, NSA_GROUPS),
        in_specs=[pl.BlockSpec((1, 1, nr, half), lambda bi, g: (bi, g + NSA_GROUPS, 0, 0))] + common,
        out_specs=pl.BlockSpec((1, 1, hd, nr), lambda bi, g: (bi, g, 0, 0)),
        out_shape=jax.ShapeDtypeStruct((b, NSA_GROUPS, hd, nr), BF16),
        name="compress_v",
    )(r, pe, w1, w2_v)
    return kc, vc


def _rank_select(v, n_rows, topn):
    sub = 8
    ng = n_rows // sub
    groups = [v[g * sub:(g + 1) * sub, :] for g in range(ng)]
    cnt = [jnp.zeros(groups[0].shape, F32) for _ in range(ng)]
    jloc = lax.broadcasted_iota(jnp.int32, groups[0].shape, 0)
    for i in range(n_rows):
        vi = v[i:i + 1, :]
        gi = i // sub
        for g in range(ng):
            if g < gi:
                one = jnp.where(vi > groups[g], 1.0, 0.0)
            elif g > gi:
                one = jnp.where(vi >= groups[g], 1.0, 0.0)
            else:
                ge = jnp.where(vi >= groups[g], 1.0, 0.0)
                gt = jnp.where(vi > groups[g], 1.0, 0.0)
                one = jnp.where(jloc > i - gi * sub, ge, gt)
            cnt[g] = cnt[g] + one
    return jnp.concatenate(cnt, axis=0) < float(topn)


def _nsa_cmp_kernel(q_ref, kc_ref, vc_ref, g_ref, ov_ref, ocmp_ref, selb_ref, s_ref, m_ref, v_ref):
    i = pl.program_id(2)
    g = pl.program_id(1)
    kc = kc_ref[0, 0]
    vc = vc_ref[0, 0]
    nc = kc.shape[0]
    tq = q_ref.shape[-1]
    n_idx = lax.broadcasted_iota(jnp.int32, (nc, tq), 0)
    t = lax.broadcasted_iota(jnp.int32, (nc, tq), 1) + i * tq
    bias = jnp.where(t >= n_idx * CMP_STRIDE + (CMP_LEN - 1), 0.0, NEG)
    any_valid = lax.broadcasted_iota(jnp.int32, (1, tq), 1) + i * tq >= CMP_LEN - 1
    for p in range(NSA_HPG):
        s = _dot(kc, q_ref[0, p]) + bias
        s_ref[p] = s
        m_ref[p] = jnp.max(s, axis=0, keepdims=True)
    psum = jnp.zeros((nc, tq), F32)
    for p in range(NSA_HPG):
        e = jnp.exp2(s_ref[p] - m_ref[p])
        l = jnp.sum(e, axis=0, keepdims=True)
        rinv = jnp.where(any_valid, 1.0 / jnp.maximum(l, 1e-30), 0.0)
        gate = g_ref[0, pl.ds(g * NSA_HPG + p, 1), :]
        ocmp_ref[0, p] = _dot(vc, e.astype(BF16)) * (rinv * gate)
        psum = psum + e * rinv
    p_hi, p_lo = _hi_lo(psum)
    imp = _dot(ov_ref[...], p_hi) + _dot(ov_ref[...], p_lo)
    n_slc = imp.shape[0]
    blk = lax.broadcasted_iota(jnp.int32, (n_slc, tq), 0)
    cur = (lax.broadcasted_iota(jnp.int32, (n_slc, tq), 1) + i * tq) // SLC_BLOCK
    forced = (blk == 0) | (blk == cur) | (blk == cur - 1)
    v = jnp.where(forced, 3e38, imp)
    v_ref[...] = jnp.where(blk <= cur, v, NEG)

    sub = 8
    rows_per_tile = tq // SLC_BLOCK
    for ng in range(1, n_slc // sub + 1):
        lo_tile = -(-((ng - 1) * sub + 1) // rows_per_tile) - 1
        hi_tile = (ng * sub) // rows_per_tile - 1
        if hi_tile < max(lo_tile, 0):
            continue

        @pl.when((i >= lo_tile) & (i <= hi_tile))
        def _(ng=ng):
            n = ng * sub
            v = v_ref[0:n, :]
            sel = _rank_select(v, n, SLC_TOPN)
            selb_ref[0, 0, 0:n, :] = jnp.where(sel & (v > 0.5 * NEG), 0.0, NEG)
            if n < n_slc:
                selb_ref[0, 0, n:, :] = jnp.full((n_slc - n, tq), NEG, F32)


def _nsa_cmp(qa, kc, vc, gt):
    b, _, _, s = qa.shape
    nq = s // TILE
    nc = kc.shape[2]
    n_slc = s // SLC_BLOCK
    cs = np.arange(nc) * CMP_STRIDE
    ss = np.arange(n_slc) * SLC_BLOCK
    ov = ((cs[None, :] <= ss[:, None] + SLC_BLOCK - 1) & (cs[None, :] + CMP_LEN - 1 >= ss[:, None]))
    ov[:, nc - 1] = False
    ov = jnp.asarray(ov.astype(np.float32), BF16)
    return pl.pallas_call(
        _nsa_cmp_kernel,
        grid=(b, NSA_GROUPS, nq),
        in_specs=[
            pl.BlockSpec((1, NSA_HPG, Q_ROWS, TILE), lambda bi, g, i: (bi, g, 0, i)),
            pl.BlockSpec((1, 1, nc, 4 * HEAD_DIM), lambda bi, g, i: (bi, g, 0, 0)),
            pl.BlockSpec((1, 1, HEAD_DIM, nc), lambda bi, g, i: (bi, g, 0, 0)),
            pl.BlockSpec((1, 32, TILE), lambda bi, g, i: (bi, 0, i)),
            pl.BlockSpec((n_slc, nc), lambda bi, g, i: (0, 0)),
        ],
        out_specs=(
            pl.BlockSpec((1, NSA_HPG, HEAD_DIM, TILE), lambda bi, g, i: (bi, g, 0, i)),
            pl.BlockSpec((1, 1, n_slc, TILE), lambda bi, g, i: (bi, g, 0, i)),
        ),
        out_shape=(
            jax.ShapeDtypeStruct((b, NSA_HEADS, HEAD_DIM, s), F32),
            jax.ShapeDtypeStruct((b, NSA_GROUPS, n_slc, s), F32),
        ),
        scratch_shapes=[pltpu.VMEM((NSA_HPG, nc, TILE), F32),
                        pltpu.VMEM((NSA_HPG, 1, TILE), F32),
                        pltpu.VMEM((n_slc, TILE), F32)],
        name="nsa_cmp",
    )(qa, kc, vc, gt, ov)


def _score_step(k_rows, q, bias, s_ref, h, slot, m_ref, midx, first):
    n = k_rows.shape[0] // TILE
    s = _dot(k_rows, q)
    if bias is not None:
        s = s + bias
    s_ref[h, pl.ds(slot, n)] = s.reshape(n, TILE, s.shape[-1])
    mt = jnp.max(s, axis=0, keepdims=True)
    m_ref[midx] = mt if first else jnp.maximum(m_ref[midx], mt)


def _value_step(v_tiles, s_ref, h, slot, m_ref, midx, acc_ref, aidx, first):
    n = len(v_tiles)
    p = jnp.exp2(s_ref[h, pl.ds(slot, n)] - m_ref[midx]).astype(BF16)
    pv = _dot(v_tiles[0], p[0])
    for j in range(1, n):
        pv = pv + _dot(v_tiles[j], p[j])
    acc_ref[aidx] = pv if first else acc_ref[aidx] + pv


def _interleave(*stages):
    for k in range(max(len(s) for s in stages)):
        for s in stages:
            if k < len(s) and s[k] is not None:
                s[k]()


def _rows2(row0, row1, tq):
    return jnp.concatenate([jnp.broadcast_to(row0, (TILE, tq)), jnp.broadcast_to(row1, (TILE, tq))], axis=0)


def _finish(acc_ref, idx):
    acc = acc_ref[idx]
    return acc[0:HEAD_DIM] / acc[HEAD_DIM:HEAD_DIM + 1]


def _tri_biases(tk, tq):
    r = lax.broadcasted_iota(jnp.int32, (tk, tq), 0)
    c = lax.broadcasted_iota(jnp.int32, (tk, tq), 1)
    causal = jnp.where(r > c, NEG, 0.0)
    far = jnp.where(r > c, 0.0, NEG)
    return causal, far


def _nsa_attn_kernel(q_ref, ksl_ref, vsl_ref, kwn_ref, vwn_ref, selb_ref, g_ref, ocmp_ref,
                     o_ref, s0_ref, sw0_ref, m0_ref, s1_ref, sw1_ref, m1_ref, acc_ref):
    i = pl.program_id(2)
    g = pl.program_id(1)
    tq = q_ref.shape[-1]
    nt = s0_ref.shape[1] - 1
    causal, far = _tri_biases(TILE, tq)
    nblk = TILE // SLC_BLOCK
    heads = range(NSA_HPG)
    w = NSA_HPG
    q = lambda p: q_ref[0, p, 0:K_COLS, :]
    banks = ((s0_ref, sw0_ref, m0_ref), (s1_ref, sw1_ref, m1_ref))

    def sel_bias(kt, valid):
        rows = []
        for j in range(nblk):
            row = selb_ref[0, 0, pl.ds(kt * nblk + j, 1), :]
            if valid is not None:
                row = jnp.where(valid, row, NEG)
            rows.append(jnp.broadcast_to(row, (SLC_BLOCK, tq)))
        return jnp.concatenate(rows, axis=0)

    def k_rows(ref, start, n):
        return ref[0, 0, pl.ds(pl.multiple_of(start * TILE, TILE), n * TILE), :]

    def scores_head(t, b):
        sb, swb, mb = b
        bias = sel_bias(t, None) + causal
        k_diag = k_rows(ksl_ref, t, 1)
        t1 = jnp.maximum(t - 1, 0)
        t2 = jnp.maximum(t - 2, 0)
        bias1 = jnp.where(t >= 1, 0.0, NEG)
        bias2 = far + jnp.where(t >= 2, 0.0, NEG)
        kw0, kw1, kw2 = k_rows(kwn_ref, t, 1), k_rows(kwn_ref, t1, 1), k_rows(kwn_ref, t2, 1)

        def head(p):
            _score_step(k_diag, q(p), bias, sb, p, nt, mb, p, True)
            _score_step(kw0, q(p), causal, swb, p, 0, mb, w + p, True)
            _score_step(kw1, q(p), bias1, swb, p, 1, mb, w + p, False)
            _score_step(kw2, q(p), bias2, swb, p, 2, mb, w + p, False)
        return [functools.partial(head, p) for p in heads]

    def scores_pair(t, c, b):
        bias = jnp.concatenate([sel_bias(2 * c, None), sel_bias(2 * c + 1, 2 * c + 1 < t)], axis=0)
        k2 = k_rows(ksl_ref, 2 * c, 2)
        return [functools.partial(_score_step, k2, q(p), bias, b[0], p, 2 * c, b[2], p, False) for p in heads]

    def values_head(t, b):
        sb, swb, mb = b
        t1 = jnp.maximum(t - 1, 0)
        t2 = jnp.maximum(t - 2, 0)
        v_diag = [vsl_ref[0, 0, t]]
        v_win = [vwn_ref[0, 0, t], vwn_ref[0, 0, t1], vwn_ref[0, 0, t2]]

        def head(p):
            _value_step(v_diag, sb, p, nt, mb, p, acc_ref, p, True)
            _value_step(v_win, swb, p, 0, mb, w + p, acc_ref, w + p, True)
        return [functools.partial(head, p) for p in heads]

    def values_pair(c, b):
        v2 = [vsl_ref[0, 0, 2 * c], vsl_ref[0, 0, 2 * c + 1]]
        return [functools.partial(_value_step, v2, b[0], p, 2 * c, b[2], p, acc_ref, p, False) for p in heads]

    def finish():
        for p in heads:
            row = g * NSA_HPG + p
            g_slc = g_ref[0, pl.ds(NSA_HEADS + row, 1), :]
            g_win = g_ref[0, pl.ds(2 * NSA_HEADS + row, 1), :]
            o_ref[0, p] = ocmp_ref[0, p] + g_slc * _finish(acc_ref, p) + g_win * _finish(acc_ref, w + p)

    @pl.when(i == 0)
    def _():
        _interleave(scores_head(i, banks[0]))

    def steady(cur, prv, odd):
        stages = [scores_head(i, cur), values_head(i - 1, prv)]
        if odd:
            stages.insert(1, scores_pair(i, i // 2, cur))
        _interleave(*stages)

        npair = i // 2

        def body2(d, carry):
            _interleave(scores_pair(i, 2 * d, cur), [None] + values_pair(2 * d, prv),
                        scores_pair(i, 2 * d + 1, cur), [None] + values_pair(2 * d + 1, prv))
            return carry
        lax.fori_loop(0, npair // 2, body2, 0)

        def body1(c, carry):
            _interleave(scores_pair(i, c, cur), [None] + values_pair(c, prv))
            return carry
        lax.fori_loop(npair - lax.rem(npair, 2), npair, body1, 0)
        finish()

    for par in (0, 1):
        @pl.when((i >= 1) & (i < nt) & (lax.rem(i, 2) == par))
        def _():
            steady(banks[par], banks[1 - par], par == 1)

    @pl.when(i == nt)
    def _():
        _interleave(values_head(i - 1, banks[1]))

        def body(c, carry):
            _interleave(values_pair(c, banks[1]))
            return carry
        lax.fori_loop(0, i // 2, body, 0)
        finish()


def _nsa_attn(qa, kaug, vt, selb, gt, ocmp):
    b, _, _, s = qa.shape
    nq = s // TILE
    n_slc = s // SLC_BLOCK
    assert WINDOW == 2 * TILE and TILE % SLC_BLOCK == 0 and nq % 2 == 0
    kspec = lambda off: pl.BlockSpec((1, 1, s, K_COLS), lambda bi, g, i: (bi, g + off, 0, 0))
    vspec = lambda off: pl.BlockSpec((1, 1, nq, V_ROWS, TILE), lambda bi, g, i: (bi, g + off, 0, 0, 0))
    p1 = lambda i: jnp.minimum(i, nq - 1)
    p2 = lambda i: jnp.maximum(i - 1, 0)
    return pl.pallas_call(
        _nsa_attn_kernel,
        grid=(b, NSA_GROUPS, nq + 1),
        in_specs=[
            pl.BlockSpec((1, NSA_HPG, Q_ROWS, TILE), lambda bi, g, i: (bi, g, 0, p1(i))),
            kspec(0), vspec(0), kspec(NSA_GROUPS), vspec(NSA_GROUPS),
            pl.BlockSpec((1, 1, n_slc, TILE), lambda bi, g, i: (bi, g, 0, p1(i))),
            pl.BlockSpec((1, 32, TILE), lambda bi, g, i: (bi, 0, p2(i))),
            pl.BlockSpec((1, NSA_HPG, HEAD_DIM, TILE), lambda bi, g, i: (bi, g, 0, p2(i))),
        ],
        out_specs=pl.BlockSpec((1, NSA_HPG, HEAD_DIM, TILE), lambda bi, g, i: (bi, g, 0, p2(i))),
        out_shape=jax.ShapeDtypeStruct((b, NSA_HEADS, HEAD_DIM, s), F32),
        scratch_shapes=2 * [pltpu.VMEM((NSA_HPG, nq + 1, TILE, TILE), F32),
                            pltpu.VMEM((NSA_HPG, 3, TILE, TILE), F32),
                            pltpu.VMEM((2 * NSA_HPG, 1, TILE), F32)]
                       + [pltpu.VMEM((2 * NSA_HPG, V_ROWS, TILE), F32)],
        compiler_params=pltpu.CompilerParams(vmem_limit_bytes=VMEM_LIMIT),
        name="nsa_attn",
    )(qa, kaug, vt, kaug, vt, selb, gt, ocmp)


MOBA_HPS = 4


def _moba_kernel(q_ref, k_ref, v_ref, kbar_ref, o_ref, s0_ref, m0_ref, s1_ref, m1_ref, acc_ref, sb_ref):
    i = pl.program_id(2)
    tq = q_ref.shape[-1]
    nt = s0_ref.shape[1] - 1
    causal, _ = _tri_biases(TILE, tq)
    heads = range(MOBA_HPS)
    q = lambda h: q_ref[0, h, 0:K_COLS, :]
    banks = ((s0_ref, m0_ref), (s1_ref, m1_ref))

    def k_rows(h, start, n):
        return k_ref[0, h, pl.ds(pl.multiple_of(start * TILE, TILE), n * TILE), :]

    def scores_head(t, b):
        def head(h):
            kb_hi, kb_lo = _hi_lo(kbar_ref[0, h])
            gate = (_dot(kb_hi, q_ref[0, h, 0:64, :]) + _dot(kb_hi, q_ref[0, h, 128:192, :])
                    + _dot(kb_lo, q_ref[0, h, 192:256, :]))
            nb = gate.shape[0]
            past = lax.broadcasted_iota(jnp.int32, (nb, tq), 0) < t
            sel = _rank_select(jnp.where(past, gate, NEG), nb, MOBA_TOPK)
            sb_ref[h] = jnp.where(sel & past, 0.0, NEG)
            _score_step(k_rows(h, t, 1), q(h), causal, b[0], h, nt, b[1], h, True)
        return [functools.partial(head, h) for h in heads]

    def scores_pair(c, b):
        def head(h):
            bias = _rows2(sb_ref[h, pl.ds(2 * c, 1), :], sb_ref[h, pl.ds(2 * c + 1, 1), :], tq)
            _score_step(k_rows(h, 2 * c, 2), q(h), bias, b[0], h, 2 * c, b[1], h, False)
        return [functools.partial(head, h) for h in heads]

    def values_head(t, b):
        def head(h):
            _value_step([v_ref[0, h, t]], b[0], h, nt, b[1], h, acc_ref, h, True)
        return [functools.partial(head, h) for h in heads]

    def values_pair(c, b):
        def head(h):
            _value_step([v_ref[0, h, 2 * c], v_ref[0, h, 2 * c + 1]], b[0], h, 2 * c, b[1], h, acc_ref, h, False)
        return [functools.partial(head, h) for h in heads]

    def finish():
        for h in heads:
            o_ref[0, h] = _finish(acc_ref, h)

    @pl.when(i == 0)
    def _():
        _interleave(scores_head(i, banks[0]))

    def steady(cur, prv, odd):
        stages = [scores_head(i, cur), values_head(i - 1, prv)]
        if odd:
            stages.insert(1, scores_pair(i // 2, cur))
        _interleave(*stages)

        npair = i // 2

        def body2(d, carry):
            _interleave(scores_pair(2 * d, cur), [None] + values_pair(2 * d, prv),
                        scores_pair(2 * d + 1, cur), [None] + values_pair(2 * d + 1, prv))
            return carry
        lax.fori_loop(0, npair // 2, body2, 0)

        def body1(c, carry):
            _interleave(scores_pair(c, cur), [None] + values_pair(c, prv))
            return carry
        lax.fori_loop(npair - lax.rem(npair, 2), npair, body1, 0)
        finish()

    for par in (0, 1):
        @pl.when((i >= 1) & (i < nt) & (lax.rem(i, 2) == par))
        def _():
            steady(banks[par], banks[1 - par], par == 1)

    @pl.when(i == nt)
    def _():
        _interleave(values_head(i - 1, banks[1]))

        def body(c, carry):
            _interleave(values_pair(c, banks[1]))
            return carry
        lax.fori_loop(0, i // 2, body, 0)
        finish()


def _moba(qb, kaug, vt, kbar):
    b, _, _, s = qb.shape
    nq = s // TILE
    nb = kbar.shape[2]
    assert nq % 2 == 0 and nb >= nq
    hps = MOBA_HPS
    off = 2 * NSA_GROUPS // hps
    p1 = lambda i: jnp.minimum(i, nq - 1)
    p2 = lambda i: jnp.maximum(i - 1, 0)
    return pl.pallas_call(
        _moba_kernel,
        grid=(b, MOBA_HEADS // hps, nq + 1),
        in_specs=[
            pl.BlockSpec((1, hps, Q_ROWS, TILE), lambda bi, h, i: (bi, h, 0, p1(i))),
            pl.BlockSpec((1, hps, s, K_COLS), lambda bi, h, i: (bi, h + off, 0, 0)),
            pl.BlockSpec((1, hps, nq, V_ROWS, TILE), lambda bi, h, i: (bi, h + off, 0, 0, 0)),
            pl.BlockSpec((1, hps, nb, HEAD_DIM), lambda bi, h, i: (bi, h, 0, 0)),
        ],
        out_specs=pl.BlockSpec((1, hps, HEAD_DIM, TILE), lambda bi, h, i: (bi, h, 0, p2(i))),
        out_shape=jax.ShapeDtypeStruct((b, MOBA_HEADS, HEAD_DIM, s), F32),
        scratch_shapes=2 * [pltpu.VMEM((hps, nq + 1, TILE, TILE), F32),
                            pltpu.VMEM((hps, 1, TILE), F32)]
                       + [pltpu.VMEM((hps, V_ROWS, TILE), F32),
                          pltpu.VMEM((hps, nb, TILE), F32)],
        compiler_params=pltpu.CompilerParams(vmem_limit_bytes=VMEM_LIMIT),
        name="moba",
    )(qb, kaug, vt, kbar)


def _out_kernel(oa_ref, ob_ref, sz_ref, sgm_ref, x_ref, wa_ref, wb_ref, wo_ref, g_ref, o_ref):
    d = x_ref.shape[-1]
    w = NSA_HEADS * HEAD_DIM
    ya = (oa_ref[0] * sz_ref[0, 0:w, :].astype(F32)).T.astype(BF16)
    yb = (ob_ref[0] * sz_ref[0, w:, :].astype(F32)).T.astype(BF16)
    sg = sgm_ref[0].astype(F32)
    m = sg[:, 0:d] * _dot(ya, wa_ref[...]) + sg[:, d:] * _dot(yb, wb_ref[...])
    r = _dot(m.astype(BF16), wo_ref[...])
    ms = jnp.mean(r * r, axis=-1, keepdims=True)
    o_ref[0] = x_ref[0] + r * lax.rsqrt(ms + RMS_EPS) * g_ref[...]


def _output(oa, ob, sz, sgm, x, w_a, w_b, w_o, norm_post):
    b, s, d = x.shape
    nt = s // TILE
    w = oa.shape[1]
    const = lambda *shape: pl.BlockSpec(shape, lambda bi, r: (0,) * len(shape))
    return pl.pallas_call(
        _out_kernel,
        grid=(b, nt),
        in_specs=[
            pl.BlockSpec((1, w, TILE), lambda bi, r: (bi, 0, r)),
            pl.BlockSpec((1, w, TILE), lambda bi, r: (bi, 0, r)),
            pl.BlockSpec((1, 2 * w, TILE), lambda bi, r: (bi, 0, r)),
            pl.BlockSpec((1, TILE, 2 * d), lambda bi, r: (bi, r, 0)),
            pl.BlockSpec((1, TILE, d), lambda bi, r: (bi, r, 0)),
            const(w, d), const(w, d), const(d, d), const(1, d),
        ],
        out_specs=pl.BlockSpec((1, TILE, d), lambda bi, r: (bi, r, 0)),
        out_shape=jax.ShapeDtypeStruct((b, s, d), F32),
        compiler_params=pltpu.CompilerParams(vmem_limit_bytes=VMEM_LIMIT),
        name="out_proj",
    )(oa, ob, sz, sgm, x, w_a.astype(BF16), w_b.astype(BF16), w_o.astype(BF16), norm_post.reshape(1, d))


def _layer(x, norm_pre, w_in, pe_k, w1_k, w2_k, pe_v, w1_v, w2_v, w_a, w_b, w_o, norm_post):
    b, s, d = x.shape
    assert s % TILE == 0 and TILE == MOBA_BLOCK
    qa, qb, vt, sz, gt, kcv, kaug, kbar, sgm = _project(x, norm_pre, w_in)
    kc, vc = _compress(kcv, pe_k, w1_k, w2_k, pe_v, w1_v, w2_v)
    ocmp, selb = _nsa_cmp(qa, kc, vc, gt)
    oa = _nsa_attn(qa, kaug, vt, selb, gt, ocmp)
    nt = s // TILE
    kbar = kbar.reshape(b, nt, MOBA_HEADS, HEAD_DIM).transpose(0, 2, 1, 3)
    nb = max(16, nt)
    kbar = jnp.pad(kbar, ((0, 0), (0, 0), (0, nb - nt), (0, 0)))
    ob = _moba(qb, kaug, vt, kbar)
    oa = oa.reshape(b, NSA_HEADS * HEAD_DIM, s)
    ob = ob.reshape(b, MOBA_HEADS * HEAD_DIM, s)
    return _output(oa, ob, sz, sgm, x, w_a, w_b, w_o, norm_post)


def kernel(x, norm_pre, w_in, cmp_pe_k, cmp_w1_k, cmp_w2_k, cmp_pe_v, cmp_w1_v, cmp_w2_v,
           w_branch_a, w_branch_b, w_o, norm_post):
    h = x
    for l in range(norm_pre.shape[0]):
        h = _layer(h, norm_pre[l], w_in[l], cmp_pe_k[l], cmp_w1_k[l], cmp_w2_k[l],
                   cmp_pe_v[l], cmp_w1_v[l], cmp_w2_v[l], w_branch_a[l], w_branch_b[l],
                   w_o[l], norm_post[l])
    return h
```

```python
import functools

import numpy as np
import jax
import jax.numpy as jnp
from jax import lax
from jax.experimental import pallas as pl
from jax.experimental.pallas import tpu as pltpu

F32 = jnp.float32
BF16 = jnp.bfloat16

HEAD_DIM = 64
NSA_HEADS = 8
NSA_GROUPS = 2
NSA_HPG = NSA_HEADS // NSA_GROUPS
CMP_LEN = 32
CMP_STRIDE = 16
CMP_HIDDEN = 2 * HEAD_DIM
SLC_BLOCK = 64
SLC_TOPN = 16
WINDOW = 512
MOBA_HEADS = 8
MOBA_BLOCK = 256
MOBA_TOPK = 3
TOTAL_HEADS = NSA_HEADS + MOBA_HEADS
RMS_EPS = 1e-6

TILE = 256
V_ROWS = 80
Q_ROWS = 256
K_COLS = 128
NEG = -1e30
LOG2E = float(np.log2(np.e))
QSCALE = float(HEAD_DIM ** -0.5) * LOG2E
VMEM_LIMIT = 56 * 1024 * 1024


def _split3(v):
    v = np.asarray(v, np.float32)
    rnd = lambda a: a.astype(BF16).astype(np.float32)
    hi = rnd(v)
    mid = rnd(v - hi)
    lo = rnd(v - hi - mid)
    return hi, mid, lo


def _alibi_slopes():
    s = (2.0 ** (-8.0 * np.arange(1, TOTAL_HEADS + 1) / TOTAL_HEADS)).astype(np.float32)
    return np.concatenate([s[0::2], s[1::2]])


def _q_aug_rows(width):
    sl = (_alibi_slopes().astype(np.float64) * LOG2E).astype(np.float32)
    hi, mid, lo = _split3(sl)
    a = np.zeros((TOTAL_HEADS, HEAD_DIM, width), np.float32)
    for r, part in enumerate((hi, hi, mid, mid, lo, lo)):
        a[:, r, :] = part[:, None]
    return jnp.asarray(a, BF16)


def _pos_aug_cols(pos):
    pos = np.asarray(pos, np.int64)
    p_hi = (pos // 64 * 64).astype(np.float32)
    p_lo = (pos % 64).astype(np.float32)
    a = np.zeros((pos.shape[0], HEAD_DIM), np.float32)
    for c in range(3):
        a[:, 2 * c] = p_hi
        a[:, 2 * c + 1] = p_lo
    return jnp.asarray(a)


def _sigmoid(x):
    return 1.0 / (1.0 + jnp.exp(-x))


def _hi_lo(x):
    hi = x.astype(BF16)
    lo = (x - hi.astype(F32)).astype(BF16)
    return hi, lo


def _dot(a, b):
    return jnp.dot(a, b, preferred_element_type=F32)


def _dot_nt(a, b):
    return lax.dot_general(a, b, (((1,), (1,)), ((), ())), preferred_element_type=F32)


def _dot3(a, b):
    a_hi, a_lo = _hi_lo(a)
    b_hi, b_lo = _hi_lo(b)
    return _dot(a_hi, b_hi) + _dot(a_hi, b_lo) + _dot(a_lo, b_hi)


N_VT = 12
N_KA = 12
CT_Q = 0
CT_V = 1024
CT_Z = CT_V + N_VT * HEAD_DIM
CT_G = CT_Z + 1024
CT_ROWS = CT_G + 32
CN_KCV = 0
CN_KA = 256
CN_MERGE = CN_KA + N_KA * HEAD_DIM


def _proj_kernel(x_ref, g_ref, wt_ref, wn_ref, qaug_ref, pos_ref,
                 qa_ref, qb_ref, vt_ref, sz_ref, gt_ref, kcv_ref, kaug_ref, kbar_ref, sgm_ref):
    x = x_ref[0]
    tr = x.shape[0]
    ms = jnp.mean(x * x, axis=-1, keepdims=True)
    u = (x * lax.rsqrt(ms + RMS_EPS) * g_ref[...]).astype(BF16)

    acc_t = _dot_nt(wt_ref[...], u)
    for h in range(TOTAL_HEADS):
        q = acc_t[h * HEAD_DIM:(h + 1) * HEAD_DIM] * QSCALE
        hi, lo = _hi_lo(q)
        ref = qa_ref if h < NSA_HEADS else qb_ref
        hh = h % NSA_HEADS
        ref[0, hh, 0:64, :] = hi
        ref[0, hh, 64:128, :] = qaug_ref[h]
        ref[0, hh, 128:192, :] = lo
        ref[0, hh, 192:256, :] = hi
    ones_rows = (lax.broadcasted_iota(jnp.int32, (V_ROWS - HEAD_DIM, tr), 0) == 0).astype(F32)
    for j in range(N_VT):
        v = acc_t[CT_V + j * HEAD_DIM:CT_V + (j + 1) * HEAD_DIM]
        vt_ref[0, j, 0] = jnp.concatenate([v, ones_rows], axis=0).astype(BF16)
    z = acc_t[CT_Z:CT_G]
    sz_ref[0] = (z * _sigmoid(z)).astype(BF16)
    gt_ref[0] = _sigmoid(acc_t[CT_G:CT_ROWS])

    acc_n = _dot(u, wn_ref[...])
    for j in range(4):
        kcv_ref[0, j] = acc_n[:, CN_KCV + j * HEAD_DIM:CN_KCV + (j + 1) * HEAD_DIM]
    pos = pos_ref[...]
    for j in range(N_KA):
        k = acc_n[:, CN_KA + j * HEAD_DIM:CN_KA + (j + 1) * HEAD_DIM]
        kaug_ref[0, j] = jnp.concatenate([k, pos], axis=1).astype(BF16)
    kb = acc_n[:, CN_KA + 4 * HEAD_DIM:CN_MERGE]
    kbar_ref[0, 0] = jnp.mean(kb, axis=0, keepdims=True)
    sgm_ref[0] = _sigmoid(acc_n[:, CN_MERGE:]).astype(BF16)


def _proj_weights(w_in, d_model):
    o_q = 0
    o_kv = o_q + NSA_HEADS * HEAD_DIM
    o_g = o_kv + 6 * NSA_GROUPS * HEAD_DIM
    o_za = o_g + 3 * NSA_HEADS
    o_b = o_za + NSA_HEADS * HEAD_DIM
    o_zb = o_b + 3 * MOBA_HEADS * HEAD_DIM
    o_m = o_zb + MOBA_HEADS * HEAD_DIM
    kv = lambda j: w_in[:, o_kv + j * 128:o_kv + (j + 1) * 128]
    qb = w_in[:, o_b:o_b + 512]
    kb = w_in[:, o_b + 512:o_b + 1024]
    vb = w_in[:, o_b + 1024:o_b + 1536]
    wt = jnp.concatenate([
        w_in[:, o_q:o_kv], qb, kv(3), kv(5), vb,
        w_in[:, o_za:o_b], w_in[:, o_zb:o_m], w_in[:, o_g:o_za],
        jnp.zeros((d_model, CT_ROWS - CT_G - 3 * NSA_HEADS), w_in.dtype)], axis=1)
    wn = jnp.concatenate([kv(0), kv(1), kv(2), kv(4), kb, w_in[:, o_m:]], axis=1)
    return wt.T.astype(BF16), wn.astype(BF16)


def _project(x, norm_pre, w_in):
    b, s, d = x.shape
    nt = s // TILE
    wt, wn = _proj_weights(w_in, d)
    cn = wn.shape[1]
    qaug = _q_aug_rows(TILE)
    pos = _pos_aug_cols(np.arange(s))
    const = lambda *shape: pl.BlockSpec(shape, lambda bi, r: (0,) * len(shape))
    out_shape = (
        jax.ShapeDtypeStruct((b, NSA_HEADS, Q_ROWS, s), BF16),
        jax.ShapeDtypeStruct((b, MOBA_HEADS, Q_ROWS, s), BF16),
        jax.ShapeDtypeStruct((b, N_VT, nt, V_ROWS, TILE), BF16),
        jax.ShapeDtypeStruct((b, 1024, s), BF16),
        jax.ShapeDtypeStruct((b, 32, s), F32),
        jax.ShapeDtypeStruct((b, 4, s, HEAD_DIM), F32),
        jax.ShapeDtypeStruct((b, N_KA, s, K_COLS), BF16),
        jax.ShapeDtypeStruct((b, nt, 1, MOBA_HEADS * HEAD_DIM), F32),
        jax.ShapeDtypeStruct((b, s, 2 * d), BF16),
    )
    out_specs = (
        pl.BlockSpec((1, NSA_HEADS, Q_ROWS, TILE), lambda bi, r: (bi, 0, 0, r)),
        pl.BlockSpec((1, MOBA_HEADS, Q_ROWS, TILE), lambda bi, r: (bi, 0, 0, r)),
        pl.BlockSpec((1, N_VT, 1, V_ROWS, TILE), lambda bi, r: (bi, 0, r, 0, 0)),
        pl.BlockSpec((1, 1024, TILE), lambda bi, r: (bi, 0, r)),
        pl.BlockSpec((1, 32, TILE), lambda bi, r: (bi, 0, r)),
        pl.BlockSpec((1, 4, TILE, HEAD_DIM), lambda bi, r: (bi, 0, r, 0)),
        pl.BlockSpec((1, N_KA, TILE, K_COLS), lambda bi, r: (bi, 0, r, 0)),
        pl.BlockSpec((1, 1, 1, MOBA_HEADS * HEAD_DIM), lambda bi, r: (bi, r, 0, 0)),
        pl.BlockSpec((1, TILE, 2 * d), lambda bi, r: (bi, r, 0)),
    )
    return pl.pallas_call(
        _proj_kernel,
        grid=(b, nt),
        in_specs=[
            pl.BlockSpec((1, TILE, d), lambda bi, r: (bi, r, 0)),
            const(1, d),
            const(CT_ROWS, d),
            const(d, cn),
            const(TOTAL_HEADS, HEAD_DIM, TILE),
            pl.BlockSpec((TILE, HEAD_DIM), lambda bi, r: (r, 0)),
        ],
        out_specs=out_specs,
        out_shape=out_shape,
        compiler_params=pltpu.CompilerParams(vmem_limit_bytes=VMEM_LIMIT),
        name="proj",
    )(x, norm_pre.reshape(1, d), wt, wn, qaug, pos)


def _compress_core(x_ref, pe_ref, w1_ref, w2_ref):
    nr = x_ref.shape[2] // CMP_STRIDE
    r = jnp.concatenate([x_ref[0, 0, pl.ds(l, nr, stride=CMP_STRIDE), :] for l in range(CMP_STRIDE)], axis=1)
    a = _dot3(r + pe_ref[0], w1_ref[0])
    bm = _dot3(r + pe_ref[1], w1_ref[1])
    h = a + pltpu.roll(bm, nr - 1, axis=0)
    h = h * _sigmoid(h)
    c = _dot3(h, w2_ref[...])
    valid = lax.broadcasted_iota(jnp.int32, c.shape, 0) < nr - 1
    return jnp.where(valid, c, 0.0)


def _compress_k_kernel(r_ref, pe_ref, w1_ref, w2_ref, pos_ref, o_ref):
    c = _compress_core(r_ref, pe_ref, w1_ref, w2_ref)
    hi, lo = _hi_lo(c)
    hi = hi.astype(F32)
    o_ref[0, 0] = jnp.concatenate([hi, pos_ref[...], hi, lo.astype(F32)], axis=1).astype(BF16)


def _compress_v_kernel(r_ref, pe_ref, w1_ref, w2_ref, o_ref):
    c = _compress_core(r_ref, pe_ref, w1_ref, w2_ref)
    o_ref[0, 0] = c.T.astype(BF16)


def _compress(kcv, pe_k, w1_k, w2_k, pe_v, w1_v, w2_v):
    b, _, s, hd = kcv.shape
    nr = s // CMP_STRIDE
    half = CMP_STRIDE * hd
    prep = lambda pe, w1: (pe.reshape(2, 1, half), w1.reshape(2, half, CMP_HIDDEN))
    pos = _pos_aug_cols(np.arange(nr) * CMP_STRIDE + CMP_LEN - 1)
    const = lambda *shape: pl.BlockSpec(shape, lambda bi, g: (0,) * len(shape))
    common = [const(2, 1, half), const(2, half, CMP_HIDDEN), const(CMP_HIDDEN, hd)]
    pe, w1 = prep(pe_k, w1_k)
    kc = pl.pallas_call(
        _compress_k_kernel,
        grid=(b, NSA_GROUPS),
        in_specs=[pl.BlockSpec((1, 1, s, hd), lambda bi, g: (bi, g, 0, 0))] + common + [const(nr, hd)],
        out_specs=pl.BlockSpec((1, 1, nr, 4 * hd), lambda bi, g: (bi, g, 0, 0)),
        out_shape=jax.ShapeDtypeStruct((b, NSA_GROUPS, nr, 4 * hd), BF16),
        name="compress_k",
    )(kcv, pe, w1, w2_k, pos)
    pe, w1 = prep(pe_v, w1_v)
    vc = pl.pallas_call(
        _compress_v_kernel,
        grid=(b, NSA_GROUPS),
        in_specs=[pl.BlockSpec((1, 1, s, hd), lambda bi, g: (bi, g + NSA_GROUPS, 0, 0))] + common,
        out_specs=pl.BlockSpec((1, 1, hd, nr), lambda bi, g: (bi, g, 0, 0)),
        out_shape=jax.ShapeDtypeStruct((b, NSA_GROUPS, hd, nr), BF16),
        name="compress_v",
    )(kcv, pe, w1, w2_v)
    return kc, vc


def _rank_select(v, n_rows, topn):
    sub = 8
    ng = n_rows // sub
    groups = [v[g * sub:(g + 1) * sub, :] for g in range(ng)]
    cnt = [jnp.zeros(groups[0].shape, F32) for _ in range(ng)]
    jloc = lax.broadcasted_iota(jnp.int32, groups[0].shape, 0)
    for i in range(n_rows):
        vi = v[i:i + 1, :]
        gi = i // sub
        for g in range(ng):
            if g < gi:
                one = jnp.where(vi > groups[g], 1.0, 0.0)
            elif g > gi:
                one = jnp.where(vi >= groups[g], 1.0, 0.0)
            else:
                ge = jnp.where(vi >= groups[g], 1.0, 0.0)
                gt = jnp.where(vi > groups[g], 1.0, 0.0)
                one = jnp.where(jloc > i - gi * sub, ge, gt)
            cnt[g] = cnt[g] + one
    return jnp.concatenate(cnt, axis=0) < float(topn)


def _nsa_cmp_kernel(q_ref, kc_ref, vc_ref, g_ref, ov_ref, ocmp_ref, selb_ref, s_ref, m_ref, v_ref):
    i = pl.program_id(2)
    g = pl.program_id(1)
    kc = kc_ref[0, 0]
    vc = vc_ref[0, 0]
    nc = kc.shape[0]
    tq = q_ref.shape[-1]
    n_idx = lax.broadcasted_iota(jnp.int32, (nc, tq), 0)
    t = lax.broadcasted_iota(jnp.int32, (nc, tq), 1) + i * tq
    bias = jnp.where(t >= n_idx * CMP_STRIDE + (CMP_LEN - 1), 0.0, NEG)
    any_valid = lax.broadcasted_iota(jnp.int32, (1, tq), 1) + i * tq >= CMP_LEN - 1
    for p in range(NSA_HPG):
        s = _dot(kc, q_ref[0, p]) + bias
        s_ref[p] = s
        m_ref[p] = jnp.max(s, axis=0, keepdims=True)
    psum = jnp.zeros((nc, tq), F32)
    for p in range(NSA_HPG):
        e = jnp.exp2(s_ref[p] - m_ref[p])
        l = jnp.sum(e, axis=0, keepdims=True)
        rinv = jnp.where(any_valid, 1.0 / jnp.maximum(l, 1e-30), 0.0)
        gate = g_ref[0, pl.ds(g * NSA_HPG + p, 1), :]
        ocmp_ref[0, p] = _dot(vc, e.astype(BF16)) * (rinv * gate)
        psum = psum + e * rinv
    p_hi, p_lo = _hi_lo(psum)
    imp = _dot(ov_ref[...], p_hi) + _dot(ov_ref[...], p_lo)
    n_slc = imp.shape[0]
    blk = lax.broadcasted_iota(jnp.int32, (n_slc, tq), 0)
    cur = (lax.broadcasted_iota(jnp.int32, (n_slc, tq), 1) + i * tq) // SLC_BLOCK
    forced = (blk == 0) | (blk == cur) | (blk == cur - 1)
    v = jnp.where(forced, 3e38, imp)
    v_ref[...] = jnp.where(blk <= cur, v, NEG)

    sub = 8
    rows_per_tile = tq // SLC_BLOCK
    for ng in range(1, n_slc // sub + 1):
        lo_tile = -(-((ng - 1) * sub + 1) // rows_per_tile) - 1
        hi_tile = (ng * sub) // rows_per_tile - 1
        if hi_tile < max(lo_tile, 0):
            continue

        @pl.when((i >= lo_tile) & (i <= hi_tile))
        def _(ng=ng):
            n = ng * sub
            v = v_ref[0:n, :]
            sel = _rank_select(v, n, SLC_TOPN)
            selb_ref[0, 0, 0:n, :] = jnp.where(sel & (v > 0.5 * NEG), 0.0, NEG)
            if n < n_slc:
                selb_ref[0, 0, n:, :] = jnp.full((n_slc - n, tq), NEG, F32)


def _nsa_cmp(qa, kc, vc, gt):
    b, _, _, s = qa.shape
    nq = s // TILE
    nc = kc.shape[2]
    n_slc = s // SLC_BLOCK
    cs = np.arange(nc) * CMP_STRIDE
    ss = np.arange(n_slc) * SLC_BLOCK
    ov = ((cs[None, :] <= ss[:, None] + SLC_BLOCK - 1) & (cs[None, :] + CMP_LEN - 1 >= ss[:, None]))
    ov[:, nc - 1] = False
    ov = jnp.asarray(ov.astype(np.float32), BF16)
    return pl.pallas_call(
        _nsa_cmp_kernel,
        grid=(b, NSA_GROUPS, nq),
        in_specs=[
            pl.BlockSpec((1, NSA_HPG, Q_ROWS, TILE), lambda bi, g, i: (bi, g, 0, i)),
            pl.BlockSpec((1, 1, nc, 4 * HEAD_DIM), lambda bi, g, i: (bi, g, 0, 0)),
            pl.BlockSpec((1, 1, HEAD_DIM, nc), lambda bi, g, i: (bi, g, 0, 0)),
            pl.BlockSpec((1, 32, TILE), lambda bi, g, i: (bi, 0, i)),
            pl.BlockSpec((n_slc, nc), lambda bi, g, i: (0, 0)),
        ],
        out_specs=(
            pl.BlockSpec((1, NSA_HPG, HEAD_DIM, TILE), lambda bi, g, i: (bi, g, 0, i)),
            pl.BlockSpec((1, 1, n_slc, TILE), lambda bi, g, i: (bi, g, 0, i)),
        ),
        out_shape=(
            jax.ShapeDtypeStruct((b, NSA_HEADS, HEAD_DIM, s), F32),
            jax.ShapeDtypeStruct((b, NSA_GROUPS, n_slc, s), F32),
        ),
        scratch_shapes=[pltpu.VMEM((NSA_HPG, nc, TILE), F32),
                        pltpu.VMEM((NSA_HPG, 1, TILE), F32),
                        pltpu.VMEM((n_slc, TILE), F32)],
        name="nsa_cmp",
    )(qa, kc, vc, gt, ov)


def _score_step(k_rows, q, bias, s_ref, h, slot, m_ref, midx, first):
    n = k_rows.shape[0] // TILE
    s = _dot(k_rows, q)
    if bias is not None:
        s = s + bias
    s_ref[h, pl.ds(slot, n)] = s.reshape(n, TILE, s.shape[-1])
    mt = jnp.max(s, axis=0, keepdims=True)
    m_ref[midx] = mt if first else jnp.maximum(m_ref[midx], mt)


def _value_step(v_tiles, s_ref, h, slot, m_ref, midx, acc_ref, aidx, first):
    n = len(v_tiles)
    p = jnp.exp2(s_ref[h, pl.ds(slot, n)] - m_ref[midx]).astype(BF16)
    pv = _dot(v_tiles[0], p[0])
    for j in range(1, n):
        pv = pv + _dot(v_tiles[j], p[j])
    acc_ref[aidx] = pv if first else acc_ref[aidx] + pv


def _interleave(*stages):
    for k in range(max(len(s) for s in stages)):
        for s in stages:
            if k < len(s) and s[k] is not None:
                s[k]()


def _rows2(row0, row1, tq):
    return jnp.concatenate([jnp.broadcast_to(row0, (TILE, tq)), jnp.broadcast_to(row1, (TILE, tq))], axis=0)


def _finish(acc_ref, idx):
    acc = acc_ref[idx]
    return acc[0:HEAD_DIM] / acc[HEAD_DIM:HEAD_DIM + 1]


def _tri_biases(tk, tq):
    r = lax.broadcasted_iota(jnp.int32, (tk, tq), 0)
    c = lax.broadcasted_iota(jnp.int32, (tk, tq), 1)
    causal = jnp.where(r > c, NEG, 0.0)
    far = jnp.where(r > c, 0.0, NEG)
    return causal, far


def _nsa_attn_kernel(q_ref, ksl_ref, vsl_ref, kwn_ref, vwn_ref, selb_ref, g_ref, ocmp_ref,
                     o_ref, s0_ref, sw0_ref, m0_ref, s1_ref, sw1_ref, m1_ref, acc_ref):
    i = pl.program_id(2)
    g = pl.program_id(1)
    tq = q_ref.shape[-1]
    nt = s0_ref.shape[1] - 1
    causal, far = _tri_biases(TILE, tq)
    nblk = TILE // SLC_BLOCK
    heads = range(NSA_HPG)
    w = NSA_HPG
    q = lambda p: q_ref[0, p, 0:K_COLS, :]
    banks = ((s0_ref, sw0_ref, m0_ref), (s1_ref, sw1_ref, m1_ref))

    def sel_bias(kt, valid):
        rows = []
        for j in range(nblk):
            row = selb_ref[0, 0, pl.ds(kt * nblk + j, 1), :]
            if valid is not None:
                row = jnp.where(valid, row, NEG)
            rows.append(jnp.broadcast_to(row, (SLC_BLOCK, tq)))
        return jnp.concatenate(rows, axis=0)

    def k_rows(ref, start, n):
        return ref[0, 0, pl.ds(pl.multiple_of(start * TILE, TILE), n * TILE), :]

    def scores_head(t, b):
        sb, swb, mb = b
        bias = sel_bias(t, None) + causal
        k_diag = k_rows(ksl_ref, t, 1)
        t1 = jnp.maximum(t - 1, 0)
        t2 = jnp.maximum(t - 2, 0)
        bias1 = jnp.where(t >= 1, 0.0, NEG)
        bias2 = far + jnp.where(t >= 2, 0.0, NEG)
        kw0, kw1, kw2 = k_rows(kwn_ref, t, 1), k_rows(kwn_ref, t1, 1), k_rows(kwn_ref, t2, 1)

        def head(p):
            _score_step(k_diag, q(p), bias, sb, p, nt, mb, p, True)
            _score_step(kw0, q(p), causal, swb, p, 0, mb, w + p, True)
            _score_step(kw1, q(p), bias1, swb, p, 1, mb, w + p, False)
            _score_step(kw2, q(p), bias2, swb, p, 2, mb, w + p, False)
        return [functools.partial(head, p) for p in heads]

    def scores_pair(t, c, b):
        bias = jnp.concatenate([sel_bias(2 * c, None), sel_bias(2 * c + 1, 2 * c + 1 < t)], axis=0)
        k2 = k_rows(ksl_ref, 2 * c, 2)
        return [functools.partial(_score_step, k2, q(p), bias, b[0], p, 2 * c, b[2], p, False) for p in heads]

    def values_head(t, b):
        sb, swb, mb = b
        t1 = jnp.maximum(t - 1, 0)
        t2 = jnp.maximum(t - 2, 0)
        v_diag = [vsl_ref[0, 0, t]]
        v_win = [vwn_ref[0, 0, t], vwn_ref[0, 0, t1], vwn_ref[0, 0, t2]]

        def head(p):
            _value_step(v_diag, sb, p, nt, mb, p, acc_ref, p, True)
            _value_step(v_win, swb, p, 0, mb, w + p, acc_ref, w + p, True)
        return [functools.partial(head, p) for p in heads]

    def values_pair(c, b):
        v2 = [vsl_ref[0, 0, 2 * c], vsl_ref[0, 0, 2 * c + 1]]
        return [functools.partial(_value_step, v2, b[0], p, 2 * c, b[2], p, acc_ref, p, False) for p in heads]

    def finish():
        for p in heads:
            row = g * NSA_HPG + p
            g_slc = g_ref[0, pl.ds(NSA_HEADS + row, 1), :]
            g_win = g_ref[0, pl.ds(2 * NSA_HEADS + row, 1), :]
            o = ocmp_ref[0, p] + g_slc * _finish(acc_ref, p) + g_win * _finish(acc_ref, w + p)
            o_ref[0, p] = o.astype(o_ref.dtype)

    @pl.when(i == 0)
    def _():
        _interleave(scores_head(i, banks[0]))

    def steady(cur, prv, odd):
        stages = [scores_head(i, cur), values_head(i - 1, prv)]
        if odd:
            stages.insert(1, scores_pair(i, i // 2, cur))
        _interleave(*stages)

        npair = i // 2

        def body2(d, carry):
            _interleave(scores_pair(i, 2 * d, cur), [None] + values_pair(2 * d, prv),
                        scores_pair(i, 2 * d + 1, cur), [None] + values_pair(2 * d + 1, prv))
            return carry
        lax.fori_loop(0, npair // 2, body2, 0)

        def body1(c, carry):
            _interleave(scores_pair(i, c, cur), [None] + values_pair(c, prv))
            return carry
        lax.fori_loop(npair - lax.rem(npair, 2), npair, body1, 0)
        finish()

    for par in (0, 1):
        @pl.when((i >= 1) & (i < nt) & (lax.rem(i, 2) == par))
        def _():
            steady(banks[par], banks[1 - par], par == 1)

    @pl.when(i == nt)
    def _():
        _interleave(values_head(i - 1, banks[1]))

        def body(c, carry):
            _interleave(values_pair(c, banks[1]))
            return carry
        lax.fori_loop(0, i // 2, body, 0)
        finish()


def _nsa_attn(qa, kaug, vt, selb, gt, ocmp):
    b, _, _, s = qa.shape
    nq = s // TILE
    n_slc = s // SLC_BLOCK
    assert WINDOW == 2 * TILE and TILE % SLC_BLOCK == 0 and nq % 2 == 0
    kspec = lambda off: pl.BlockSpec((1, 1, s, K_COLS), lambda bi, g, i: (bi, g + off, 0, 0))
    vspec = lambda off: pl.BlockSpec((1, 1, nq, V_ROWS, TILE), lambda bi, g, i: (bi, g + off, 0, 0, 0))
    p1 = lambda i: jnp.minimum(i, nq - 1)
    p2 = lambda i: jnp.maximum(i - 1, 0)
    return pl.pallas_call(
        _nsa_attn_kernel,
        grid=(b, NSA_GROUPS, nq + 1),
        in_specs=[
            pl.BlockSpec((1, NSA_HPG, Q_ROWS, TILE), lambda bi, g, i: (bi, g, 0, p1(i))),
            kspec(0), vspec(0), kspec(NSA_GROUPS), vspec(NSA_GROUPS),
            pl.BlockSpec((1, 1, n_slc, TILE), lambda bi, g, i: (bi, g, 0, p1(i))),
            pl.BlockSpec((1, 32, TILE), lambda bi, g, i: (bi, 0, p2(i))),
            pl.BlockSpec((1, NSA_HPG, HEAD_DIM, TILE), lambda bi, g, i: (bi, g, 0, p2(i))),
        ],
        out_specs=pl.BlockSpec((1, NSA_HPG, HEAD_DIM, TILE), lambda bi, g, i: (bi, g, 0, p2(i))),
        out_shape=jax.ShapeDtypeStruct((b, NSA_HEADS, HEAD_DIM, s), BF16),
        scratch_shapes=2 * [pltpu.VMEM((NSA_HPG, nq + 1, TILE, TILE), F32),
                            pltpu.VMEM((NSA_HPG, 3, TILE, TILE), F32),
                            pltpu.VMEM((2 * NSA_HPG, 1, TILE), F32)]
                       + [pltpu.VMEM((2 * NSA_HPG, V_ROWS, TILE), F32)],
        compiler_params=pltpu.CompilerParams(vmem_limit_bytes=VMEM_LIMIT),
        name="nsa_attn",
    )(qa, kaug, vt, kaug, vt, selb, gt, ocmp)


MOBA_HPS = 4


def _moba_kernel(q_ref, k_ref, v_ref, kbar_ref, o_ref, s0_ref, m0_ref, s1_ref, m1_ref, acc_ref, sb_ref):
    i = pl.program_id(2)
    tq = q_ref.shape[-1]
    nt = s0_ref.shape[1] - 1
    causal, _ = _tri_biases(TILE, tq)
    heads = range(MOBA_HPS)
    q = lambda h: q_ref[0, h, 0:K_COLS, :]
    banks = ((s0_ref, m0_ref), (s1_ref, m1_ref))

    def k_rows(h, start, n):
        return k_ref[0, h, pl.ds(pl.multiple_of(start * TILE, TILE), n * TILE), :]

    def scores_head(t, b):
        def head(h):
            kb_hi, kb_lo = _hi_lo(kbar_ref[0, h])
            gate = (_dot(kb_hi, q_ref[0, h, 0:64, :]) + _dot(kb_hi, q_ref[0, h, 128:192, :])
                    + _dot(kb_lo, q_ref[0, h, 192:256, :]))
            nb = gate.shape[0]
            past = lax.broadcasted_iota(jnp.int32, (nb, tq), 0) < t
            sel = _rank_select(jnp.where(past, gate, NEG), nb, MOBA_TOPK)
            sb_ref[h] = jnp.where(sel & past, 0.0, NEG)
            _score_step(k_rows(h, t, 1), q(h), causal, b[0], h, nt, b[1], h, True)
        return [functools.partial(head, h) for h in heads]

    def scores_pair(c, b):
        def head(h):
            bias = _rows2(sb_ref[h, pl.ds(2 * c, 1), :], sb_ref[h, pl.ds(2 * c + 1, 1), :], tq)
            _score_step(k_rows(h, 2 * c, 2), q(h), bias, b[0], h, 2 * c, b[1], h, False)
        return [functools.partial(head, h) for h in heads]

    def values_head(t, b):
        def head(h):
            _value_step([v_ref[0, h, t]], b[0], h, nt, b[1], h, acc_ref, h, True)
        return [functools.partial(head, h) for h in heads]

    def values_pair(c, b):
        def head(h):
            _value_step([v_ref[0, h, 2 * c], v_ref[0, h, 2 * c + 1]], b[0], h, 2 * c, b[1], h, acc_ref, h, False)
        return [functools.partial(head, h) for h in heads]

    def finish():
        for h in heads:
            o_ref[0, h] = _finish(acc_ref, h).astype(o_ref.dtype)

    @pl.when(i == 0)
    def _():
        _interleave(scores_head(i, banks[0]))

    def steady(cur, prv, odd):
        stages = [scores_head(i, cur), values_head(i - 1, prv)]
        if odd:
            stages.insert(1, scores_pair(i // 2, cur))
        _interleave(*stages)

        npair = i // 2

        def body2(d, carry):
            _interleave(scores_pair(2 * d, cur), [None] + values_pair(2 * d, prv),
                        scores_pair(2 * d + 1, cur), [None] + values_pair(2 * d + 1, prv))
            return carry
        lax.fori_loop(0, npair // 2, body2, 0)

        def body1(c, carry):
            _interleave(scores_pair(c, cur), [None] + values_pair(c, prv))
            return carry
        lax.fori_loop(npair - lax.rem(npair, 2), npair, body1, 0)
        finish()

    for par in (0, 1):
        @pl.when((i >= 1) & (i < nt) & (lax.rem(i, 2) == par))
        def _():
            steady(banks[par], banks[1 - par], par == 1)

    @pl.when(i == nt)
    def _():
        _interleave(values_head(i - 1, banks[1]))

        def body(c, carry):
            _interleave(values_pair(c, banks[1]))
            return carry
        lax.fori_loop(0, i // 2, body, 0)
        finish()


def _moba(qb, kaug, vt, kbar):
    b, _, _, s = qb.shape
    nq = s // TILE
    nb = kbar.shape[2]
    assert nq % 2 == 0 and nb >= nq
    hps = MOBA_HPS
    off = 2 * NSA_GROUPS // hps
    p1 = lambda i: jnp.minimum(i, nq - 1)
    p2 = lambda i: jnp.maximum(i - 1, 0)
    return pl.pallas_call(
        _moba_kernel,
        grid=(b, MOBA_HEADS // hps, nq + 1),
        in_specs=[
            pl.BlockSpec((1, hps, Q_ROWS, TILE), lambda bi, h, i: (bi, h, 0, p1(i))),
            pl.BlockSpec((1, hps, s, K_COLS), lambda bi, h, i: (bi, h + off, 0, 0)),
            pl.BlockSpec((1, hps, nq, V_ROWS, TILE), lambda bi, h, i: (bi, h + off, 0, 0, 0)),
            pl.BlockSpec((1, hps, nb, HEAD_DIM), lambda bi, h, i: (bi, h, 0, 0)),
        ],
        out_specs=pl.BlockSpec((1, hps, HEAD_DIM, TILE), lambda bi, h, i: (bi, h, 0, p2(i))),
        out_shape=jax.ShapeDtypeStruct((b, MOBA_HEADS, HEAD_DIM, s), BF16),
        scratch_shapes=2 * [pltpu.VMEM((hps, nq + 1, TILE, TILE), F32),
                            pltpu.VMEM((hps, 1, TILE), F32)]
                       + [pltpu.VMEM((hps, V_ROWS, TILE), F32),
                          pltpu.VMEM((hps, nb, TILE), F32)],
        compiler_params=pltpu.CompilerParams(vmem_limit_bytes=VMEM_LIMIT),
        name="moba",
    )(qb, kaug, vt, kbar)


def _out_kernel(oa_ref, ob_ref, sz_ref, sgm_ref, x_ref, wa_ref, wb_ref, wo_ref, g_ref, o_ref):
    d = x_ref.shape[-1]
    w = NSA_HEADS * HEAD_DIM
    ya = (oa_ref[0].astype(F32) * sz_ref[0, 0:w, :].astype(F32)).T.astype(BF16)
    yb = (ob_ref[0].astype(F32) * sz_ref[0, w:, :].astype(F32)).T.astype(BF16)
    sg = sgm_ref[0].astype(F32)
    m = sg[:, 0:d] * _dot(ya, wa_ref[...]) + sg[:, d:] * _dot(yb, wb_ref[...])
    r = _dot(m.astype(BF16), wo_ref[...])
    ms = jnp.mean(r * r, axis=-1, keepdims=True)
    o_ref[0] = x_ref[0] + r * lax.rsqrt(ms + RMS_EPS) * g_ref[...]


def _output(oa, ob, sz, sgm, x, w_a, w_b, w_o, norm_post):
    b, s, d = x.shape
    nt = s // TILE
    w = oa.shape[1]
    const = lambda *shape: pl.BlockSpec(shape, lambda bi, r: (0,) * len(shape))
    return pl.pallas_call(
        _out_kernel,
        grid=(b, nt),
        in_specs=[
            pl.BlockSpec((1, w, TILE), lambda bi, r: (bi, 0, r)),
            pl.BlockSpec((1, w, TILE), lambda bi, r: (bi, 0, r)),
            pl.BlockSpec((1, 2 * w, TILE), lambda bi, r: (bi, 0, r)),
            pl.BlockSpec((1, TILE, 2 * d), lambda bi, r: (bi, r, 0)),
            pl.BlockSpec((1, TILE, d), lambda bi, r: (bi, r, 0)),
            const(w, d), const(w, d), const(d, d), const(1, d),
        ],
        out_specs=pl.BlockSpec((1, TILE, d), lambda bi, r: (bi, r, 0)),
        out_shape=jax.ShapeDtypeStruct((b, s, d), F32),
        compiler_params=pltpu.CompilerParams(vmem_limit_bytes=VMEM_LIMIT),
        name="out_proj",
    )(oa, ob, sz, sgm, x, w_a.astype(BF16), w_b.astype(BF16), w_o.astype(BF16), norm_post.reshape(1, d))


def _layer(x, norm_pre, w_in, pe_k, w1_k, w2_k, pe_v, w1_v, w2_v, w_a, w_b, w_o, norm_post):
    b, s, d = x.shape
    assert s % TILE == 0 and TILE == MOBA_BLOCK
    qa, qb, vt, sz, gt, kcv, kaug, kbar, sgm = _project(x, norm_pre, w_in)
    kc, vc = _compress(kcv, pe_k, w1_k, w2_k, pe_v, w1_v, w2_v)
    ocmp, selb = _nsa_cmp(qa, kc, vc, gt)
    oa = _nsa_attn(qa, kaug, vt, selb, gt, ocmp)
    nt = s // TILE
    kbar = kbar.reshape(b, nt, MOBA_HEADS, HEAD_DIM).transpose(0, 2, 1, 3)
    nb = max(16, nt)
    kbar = jnp.pad(kbar, ((0, 0), (0, 0), (0, nb - nt), (0, 0)))
    ob = _moba(qb, kaug, vt, kbar)
    oa = oa.reshape(b, NSA_HEADS * HEAD_DIM, s)
    ob = ob.reshape(b, MOBA_HEADS * HEAD_DIM, s)
    return _output(oa, ob, sz, sgm, x, w_a, w_b, w_o, norm_post)


def kernel(x, norm_pre, w_in, cmp_pe_k, cmp_w1_k, cmp_w2_k, cmp_pe_v, cmp_w1_v, cmp_w2_v,
           w_branch_a, w_branch_b, w_o, norm_post):
    h = x
    for l in range(norm_pre.shape[0]):
        h = _layer(h, norm_pre[l], w_in[l], cmp_pe_k[l], cmp_w1_k[l], cmp_w2_k[l],
                   cmp_pe_v[l], cmp_w1_v[l], cmp_w2_v[l], w_branch_a[l], w_branch_b[l],
                   w_o[l], norm_post[l])
    return h
```

```python
import functools

import numpy as np
import jax
import jax.numpy as jnp
from jax import lax
from jax.experimental import pallas as pl
from jax.experimental.pallas import tpu as pltpu

F32 = jnp.float32
BF16 = jnp.bfloat16

HEAD_DIM = 64
NSA_HEADS = 8
NSA_GROUPS = 2
NSA_HPG = NSA_HEADS // NSA_GROUPS
CMP_LEN = 32
CMP_STRIDE = 16
CMP_HIDDEN = 2 * HEAD_DIM
SLC_BLOCK = 64
SLC_TOPN = 16
WINDOW = 512
MOBA_HEADS = 8
MOBA_BLOCK = 256
MOBA_TOPK = 3
TOTAL_HEADS = NSA_HEADS + MOBA_HEADS
RMS_EPS = 1e-6

TILE = 256
V_ROWS = 80
Q_ROWS = 256
K_COLS = 128
NEG = -1e30
LOG2E = float(np.log2(np.e))
QSCALE = float(HEAD_DIM ** -0.5) * LOG2E
VMEM_LIMIT = 56 * 1024 * 1024


def _split3(v):
    v = np.asarray(v, np.float32)
    rnd = lambda a: a.astype(BF16).astype(np.float32)
    hi = rnd(v)
    mid = rnd(v - hi)
    lo = rnd(v - hi - mid)
    return hi, mid, lo


def _alibi_slopes():
    s = (2.0 ** (-8.0 * np.arange(1, TOTAL_HEADS + 1) / TOTAL_HEADS)).astype(np.float32)
    return np.concatenate([s[0::2], s[1::2]])


def _q_aug_rows(width):
    sl = (_alibi_slopes().astype(np.float64) * LOG2E).astype(np.float32)
    hi, mid, lo = _split3(sl)
    a = np.zeros((TOTAL_HEADS, HEAD_DIM, width), np.float32)
    for r, part in enumerate((hi, hi, mid, mid, lo, lo)):
        a[:, r, :] = part[:, None]
    return jnp.asarray(a, BF16)


def _pos_aug_cols(pos):
    pos = np.asarray(pos, np.int64)
    p_hi = (pos // 64 * 64).astype(np.float32)
    p_lo = (pos % 64).astype(np.float32)
    a = np.zeros((pos.shape[0], HEAD_DIM), np.float32)
    for c in range(3):
        a[:, 2 * c] = p_hi
        a[:, 2 * c + 1] = p_lo
    return jnp.asarray(a)


def _sigmoid(x):
    return 1.0 / (1.0 + jnp.exp(-x))


def _hi_lo(x):
    hi = x.astype(BF16)
    lo = (x - hi.astype(F32)).astype(BF16)
    return hi, lo


def _dot(a, b):
    return jnp.dot(a, b, preferred_element_type=F32)


def _dot_nt(a, b):
    return lax.dot_general(a, b, (((1,), (1,)), ((), ())), preferred_element_type=F32)


def _dot3(a, b):
    a_hi, a_lo = _hi_lo(a)
    b_hi, b_lo = _hi_lo(b)
    return _dot(a_hi, b_hi) + _dot(a_hi, b_lo) + _dot(a_lo, b_hi)


N_VT = 12
N_KA = 12
CT_Q = 0
CT_V = 1024
CT_Z = CT_V + N_VT * HEAD_DIM
CT_G = CT_Z + 1024
CT_ROWS = CT_G + 32
CN_KCV = 0
CN_KA = 256
CN_MERGE = CN_KA + N_KA * HEAD_DIM


def _proj_kernel(x_ref, g_ref, wt_ref, wn_ref, qaug_ref, pos_ref,
                 qa_ref, qb_ref, vt_ref, sz_ref, gt_ref, kcv_ref, kaug_ref, kbar_ref, sgm_ref):
    x = x_ref[0]
    tr = x.shape[0]
    ms = jnp.mean(x * x, axis=-1, keepdims=True)
    u = (x * lax.rsqrt(ms + RMS_EPS) * g_ref[...]).astype(BF16)

    acc_t = _dot_nt(wt_ref[...], u)
    for h in range(TOTAL_HEADS):
        q = acc_t[h * HEAD_DIM:(h + 1) * HEAD_DIM] * QSCALE
        hi, lo = _hi_lo(q)
        ref = qa_ref if h < NSA_HEADS else qb_ref
        hh = h % NSA_HEADS
        ref[0, hh, 0:64, :] = hi
        ref[0, hh, 64:128, :] = qaug_ref[h]
        ref[0, hh, 128:192, :] = lo
        ref[0, hh, 192:256, :] = hi
    ones_rows = (lax.broadcasted_iota(jnp.int32, (V_ROWS - HEAD_DIM, tr), 0) == 0).astype(F32)
    for j in range(N_VT):
        v = acc_t[CT_V + j * HEAD_DIM:CT_V + (j + 1) * HEAD_DIM]
        vt_ref[0, j, 0] = jnp.concatenate([v, ones_rows], axis=0).astype(BF16)
    z = acc_t[CT_Z:CT_G]
    sz_ref[0] = (z * _sigmoid(z)).astype(BF16)
    gt_ref[0] = _sigmoid(acc_t[CT_G:CT_ROWS])

    acc_n = _dot(u, wn_ref[...])
    for j in range(4):
        kcv_ref[0, j] = acc_n[:, CN_KCV + j * HEAD_DIM:CN_KCV + (j + 1) * HEAD_DIM]
    pos = pos_ref[...]
    for j in range(N_KA):
        k = acc_n[:, CN_KA + j * HEAD_DIM:CN_KA + (j + 1) * HEAD_DIM]
        kaug_ref[0, j] = jnp.concatenate([k, pos], axis=1).astype(BF16)
    kb = acc_n[:, CN_KA + 4 * HEAD_DIM:CN_MERGE]
    kbar_ref[0, 0] = jnp.mean(kb, axis=0, keepdims=True)
    sgm_ref[0] = _sigmoid(acc_n[:, CN_MERGE:]).astype(BF16)


def _proj_weights(w_in, d_model):
    o_q = 0
    o_kv = o_q + NSA_HEADS * HEAD_DIM
    o_g = o_kv + 6 * NSA_GROUPS * HEAD_DIM
    o_za = o_g + 3 * NSA_HEADS
    o_b = o_za + NSA_HEADS * HEAD_DIM
    o_zb = o_b + 3 * MOBA_HEADS * HEAD_DIM
    o_m = o_zb + MOBA_HEADS * HEAD_DIM
    kv = lambda j: w_in[:, o_kv + j * 128:o_kv + (j + 1) * 128]
    qb = w_in[:, o_b:o_b + 512]
    kb = w_in[:, o_b + 512:o_b + 1024]
    vb = w_in[:, o_b + 1024:o_b + 1536]
    wt = jnp.concatenate([
        w_in[:, o_q:o_kv], qb, kv(3), kv(5), vb,
        w_in[:, o_za:o_b], w_in[:, o_zb:o_m], w_in[:, o_g:o_za],
        jnp.zeros((d_model, CT_ROWS - CT_G - 3 * NSA_HEADS), w_in.dtype)], axis=1)
    wn = jnp.concatenate([kv(0), kv(1), kv(2), kv(4), kb, w_in[:, o_m:]], axis=1)
    return wt.T.astype(BF16), wn.astype(BF16)


def _project(x, norm_pre, w_in):
    b, s, d = x.shape
    nt = s // TILE
    wt, wn = _proj_weights(w_in, d)
    cn = wn.shape[1]
    qaug = _q_aug_rows(TILE)
    pos = _pos_aug_cols(np.arange(s))
    const = lambda *shape: pl.BlockSpec(shape, lambda bi, r: (0,) * len(shape))
    out_shape = (
        jax.ShapeDtypeStruct((b, nt, NSA_HEADS, Q_ROWS, TILE), BF16),
        jax.ShapeDtypeStruct((b, nt, MOBA_HEADS, Q_ROWS, TILE), BF16),
        jax.ShapeDtypeStruct((b, N_VT, nt, V_ROWS, TILE), BF16),
        jax.ShapeDtypeStruct((b, nt, 1024, TILE), BF16),
        jax.ShapeDtypeStruct((b, nt, 32, TILE), F32),
        jax.ShapeDtypeStruct((b, 4, s, HEAD_DIM), F32),
        jax.ShapeDtypeStruct((b, N_KA, s, K_COLS), BF16),
        jax.ShapeDtypeStruct((b, nt, 1, MOBA_HEADS * HEAD_DIM), F32),
        jax.ShapeDtypeStruct((b, s, 2 * d), BF16),
    )
    out_specs = (
        pl.BlockSpec((1, None, NSA_HEADS, Q_ROWS, TILE), lambda bi, r: (bi, r, 0, 0, 0)),
        pl.BlockSpec((1, None, MOBA_HEADS, Q_ROWS, TILE), lambda bi, r: (bi, r, 0, 0, 0)),
        pl.BlockSpec((1, N_VT, 1, V_ROWS, TILE), lambda bi, r: (bi, 0, r, 0, 0)),
        pl.BlockSpec((1, None, 1024, TILE), lambda bi, r: (bi, r, 0, 0)),
        pl.BlockSpec((1, None, 32, TILE), lambda bi, r: (bi, r, 0, 0)),
        pl.BlockSpec((1, 4, TILE, HEAD_DIM), lambda bi, r: (bi, 0, r, 0)),
        pl.BlockSpec((1, N_KA, TILE, K_COLS), lambda bi, r: (bi, 0, r, 0)),
        pl.BlockSpec((1, 1, 1, MOBA_HEADS * HEAD_DIM), lambda bi, r: (bi, r, 0, 0)),
        pl.BlockSpec((1, TILE, 2 * d), lambda bi, r: (bi, r, 0)),
    )
    return pl.pallas_call(
        _proj_kernel,
        grid=(b, nt),
        in_specs=[
            pl.BlockSpec((1, TILE, d), lambda bi, r: (bi, r, 0)),
            const(1, d),
            const(CT_ROWS, d),
            const(d, cn),
            const(TOTAL_HEADS, HEAD_DIM, TILE),
            pl.BlockSpec((TILE, HEAD_DIM), lambda bi, r: (r, 0)),
        ],
        out_specs=out_specs,
        out_shape=out_shape,
        compiler_params=pltpu.CompilerParams(vmem_limit_bytes=VMEM_LIMIT),
        name="proj",
    )(x, norm_pre.reshape(1, d), wt, wn, qaug, pos)


def _compress_core(x_ref, pe_ref, w1_ref, w2_ref):
    nr = x_ref.shape[2] // CMP_STRIDE
    r = jnp.concatenate([x_ref[0, 0, pl.ds(l, nr, stride=CMP_STRIDE), :] for l in range(CMP_STRIDE)], axis=1)
    a = _dot3(r + pe_ref[0], w1_ref[0])
    bm = _dot3(r + pe_ref[1], w1_ref[1])
    h = a + pltpu.roll(bm, nr - 1, axis=0)
    h = h * _sigmoid(h)
    c = _dot3(h, w2_ref[...])
    valid = lax.broadcasted_iota(jnp.int32, c.shape, 0) < nr - 1
    return jnp.where(valid, c, 0.0)


def _compress_k_kernel(r_ref, pe_ref, w1_ref, w2_ref, pos_ref, o_ref):
    c = _compress_core(r_ref, pe_ref, w1_ref, w2_ref)
    hi, lo = _hi_lo(c)
    hi = hi.astype(F32)
    o_ref[0, 0] = jnp.concatenate([hi, pos_ref[...], hi, lo.astype(F32)], axis=1).astype(BF16)


def _compress_v_kernel(r_ref, pe_ref, w1_ref, w2_ref, o_ref):
    c = _compress_core(r_ref, pe_ref, w1_ref, w2_ref)
    o_ref[0, 0] = c.T.astype(BF16)


def _compress(kcv, pe_k, w1_k, w2_k, pe_v, w1_v, w2_v):
    b, _, s, hd = kcv.shape
    nr = s // CMP_STRIDE
    half = CMP_STRIDE * hd
    prep = lambda pe, w1: (pe.reshape(2, 1, half), w1.reshape(2, half, CMP_HIDDEN))
    pos = _pos_aug_cols(np.arange(nr) * CMP_STRIDE + CMP_LEN - 1)
    const = lambda *shape: pl.BlockSpec(shape, lambda bi, g: (0,) * len(shape))
    common = [const(2, 1, half), const(2, half, CMP_HIDDEN), const(CMP_HIDDEN, hd)]
    pe, w1 = prep(pe_k, w1_k)
    kc = pl.pallas_call(
        _compress_k_kernel,
        grid=(b, NSA_GROUPS),
        in_specs=[pl.BlockSpec((1, 1, s, hd), lambda bi, g: (bi, g, 0, 0))] + common + [const(nr, hd)],
        out_specs=pl.BlockSpec((1, 1, nr, 4 * hd), lambda bi, g: (bi, g, 0, 0)),
        out_shape=jax.ShapeDtypeStruct((b, NSA_GROUPS, nr, 4 * hd), BF16),
        name="compress_k",
    )(kcv, pe, w1, w2_k, pos)
    pe, w1 = prep(pe_v, w1_v)
    vc = pl.pallas_call(
        _compress_v_kernel,
        grid=(b, NSA_GROUPS),
        in_specs=[pl.BlockSpec((1, 1, s, hd), lambda bi, g: (bi, g + NSA_GROUPS, 0, 0))] + common,
        out_specs=pl.BlockSpec((1, 1, hd, nr), lambda bi, g: (bi, g, 0, 0)),
        out_shape=jax.ShapeDtypeStruct((b, NSA_GROUPS, hd, nr), BF16),
        name="compress_v",
    )(kcv, pe, w1, w2_v)
    return kc, vc


def _rank_select(v, n_rows, topn):
    sub = 8
    ng = n_rows // sub
    groups = [v[g * sub:(g + 1) * sub, :] for g in range(ng)]
    cnt = [jnp.zeros(groups[0].shape, F32) for _ in range(ng)]
    jloc = lax.broadcasted_iota(jnp.int32, groups[0].shape, 0)
    for i in range(n_rows):
        vi = v[i:i + 1, :]
        gi = i // sub
        for g in range(ng):
            if g < gi:
                one = jnp.where(vi > groups[g], 1.0, 0.0)
            elif g > gi:
                one = jnp.where(vi >= groups[g], 1.0, 0.0)
            else:
                ge = jnp.where(vi >= groups[g], 1.0, 0.0)
                gt = jnp.where(vi > groups[g], 1.0, 0.0)
                one = jnp.where(jloc > i - gi * sub, ge, gt)
            cnt[g] = cnt[g] + one
    return jnp.concatenate(cnt, axis=0) < float(topn)


def _nsa_cmp_kernel(q_ref, kc_ref, vc_ref, g_ref, ov_ref, ocmp_ref, selb_ref, s_ref, m_ref, v_ref):
    i = pl.program_id(2)
    g = pl.program_id(1)
    kc = kc_ref[0, 0]
    vc = vc_ref[0, 0]
    nc = kc.shape[0]
    tq = q_ref.shape[-1]
    n_idx = lax.broadcasted_iota(jnp.int32, (nc, tq), 0)
    t = lax.broadcasted_iota(jnp.int32, (nc, tq), 1) + i * tq
    bias = jnp.where(t >= n_idx * CMP_STRIDE + (CMP_LEN - 1), 0.0, NEG)
    any_valid = lax.broadcasted_iota(jnp.int32, (1, tq), 1) + i * tq >= CMP_LEN - 1
    for p in range(NSA_HPG):
        s = _dot(kc, q_ref[0, p]) + bias
        s_ref[p] = s
        m_ref[p] = jnp.max(s, axis=0, keepdims=True)
    psum = jnp.zeros((nc, tq), F32)
    for p in range(NSA_HPG):
        e = jnp.exp2(s_ref[p] - m_ref[p])
        l = jnp.sum(e, axis=0, keepdims=True)
        rinv = jnp.where(any_valid, 1.0 / jnp.maximum(l, 1e-30), 0.0)
        gate = g_ref[0, pl.ds(g * NSA_HPG + p, 1), :]
        ocmp_ref[0, p] = _dot(vc, e.astype(BF16)) * (rinv * gate)
        psum = psum + e * rinv
    p_hi, p_lo = _hi_lo(psum)
    imp = _dot(ov_ref[...], p_hi) + _dot(ov_ref[...], p_lo)
    n_slc = imp.shape[0]
    blk = lax.broadcasted_iota(jnp.int32, (n_slc, tq), 0)
    cur = (lax.broadcasted_iota(jnp.int32, (n_slc, tq), 1) + i * tq) // SLC_BLOCK
    forced = (blk == 0) | (blk == cur) | (blk == cur - 1)
    v = jnp.where(forced, 3e38, imp)
    v_ref[...] = jnp.where(blk <= cur, v, NEG)

    sub = 8
    rows_per_tile = tq // SLC_BLOCK
    for ng in range(1, n_slc // sub + 1):
        lo_tile = -(-((ng - 1) * sub + 1) // rows_per_tile) - 1
        hi_tile = (ng * sub) // rows_per_tile - 1
        if hi_tile < max(lo_tile, 0):
            continue

        @pl.when((i >= lo_tile) & (i <= hi_tile))
        def _(ng=ng):
            n = ng * sub
            v = v_ref[0:n, :]
            sel = _rank_select(v, n, SLC_TOPN)
            selb_ref[0, 0, 0:n, :] = jnp.where(sel & (v > 0.5 * NEG), 0.0, NEG)
            if n < n_slc:
                selb_ref[0, 0, n:, :] = jnp.full((n_slc - n, tq), NEG, F32)


def _nsa_cmp(qa, kc, vc, gt):
    b, nq = qa.shape[:2]
    s = nq * TILE
    nc = kc.shape[2]
    n_slc = s // SLC_BLOCK
    cs = np.arange(nc) * CMP_STRIDE
    ss = np.arange(n_slc) * SLC_BLOCK
    ov = ((cs[None, :] <= ss[:, None] + SLC_BLOCK - 1) & (cs[None, :] + CMP_LEN - 1 >= ss[:, None]))
    ov[:, nc - 1] = False
    ov = jnp.asarray(ov.astype(np.float32), BF16)
    return pl.pallas_call(
        _nsa_cmp_kernel,
        grid=(b, NSA_GROUPS, nq),
        in_specs=[
            pl.BlockSpec((1, None, NSA_HPG, Q_ROWS, TILE), lambda bi, g, i: (bi, i, g, 0, 0)),
            pl.BlockSpec((1, 1, nc, 4 * HEAD_DIM), lambda bi, g, i: (bi, g, 0, 0)),
            pl.BlockSpec((1, 1, HEAD_DIM, nc), lambda bi, g, i: (bi, g, 0, 0)),
            pl.BlockSpec((1, None, 32, TILE), lambda bi, g, i: (bi, i, 0, 0)),
            pl.BlockSpec((n_slc, nc), lambda bi, g, i: (0, 0)),
        ],
        out_specs=(
            pl.BlockSpec((1, None, NSA_HPG, HEAD_DIM, TILE), lambda bi, g, i: (bi, i, g, 0, 0)),
            pl.BlockSpec((1, None, 1, n_slc, TILE), lambda bi, g, i: (bi, i, g, 0, 0)),
        ),
        out_shape=(
            jax.ShapeDtypeStruct((b, nq, NSA_HEADS, HEAD_DIM, TILE), F32),
            jax.ShapeDtypeStruct((b, nq, NSA_GROUPS, n_slc, TILE), F32),
        ),
        scratch_shapes=[pltpu.VMEM((NSA_HPG, nc, TILE), F32),
                        pltpu.VMEM((NSA_HPG, 1, TILE), F32),
                        pltpu.VMEM((n_slc, TILE), F32)],
        name="nsa_cmp",
    )(qa, kc, vc, gt, ov)


def _score_step(k_rows, q, bias, s_ref, h, slot, m_ref, midx, first):
    n = k_rows.shape[0] // TILE
    s = _dot(k_rows, q)
    if bias is not None:
        s = s + bias
    s_ref[h, pl.ds(slot, n)] = s.reshape(n, TILE, s.shape[-1])
    mt = jnp.max(s, axis=0, keepdims=True)
    m_ref[midx] = mt if first else jnp.maximum(m_ref[midx], mt)


def _value_step(v_tiles, s_ref, h, slot, m_ref, midx, acc_ref, aidx, first):
    n = len(v_tiles)
    p = jnp.exp2(s_ref[h, pl.ds(slot, n)] - m_ref[midx]).astype(BF16)
    pv = _dot(v_tiles[0], p[0])
    for j in range(1, n):
        pv = pv + _dot(v_tiles[j], p[j])
    acc_ref[aidx] = pv if first else acc_ref[aidx] + pv


def _interleave(*stages):
    for k in range(max(len(s) for s in stages)):
        for s in stages:
            if k < len(s) and s[k] is not None:
                s[k]()


def _rows2(row0, row1, tq):
    return jnp.concatenate([jnp.broadcast_to(row0, (TILE, tq)), jnp.broadcast_to(row1, (TILE, tq))], axis=0)


def _finish(acc_ref, idx):
    acc = acc_ref[idx]
    return acc[0:HEAD_DIM] / acc[HEAD_DIM:HEAD_DIM + 1]


def _tri_biases(tk, tq):
    r = lax.broadcasted_iota(jnp.int32, (tk, tq), 0)
    c = lax.broadcasted_iota(jnp.int32, (tk, tq), 1)
    causal = jnp.where(r > c, NEG, 0.0)
    far = jnp.where(r > c, 0.0, NEG)
    return causal, far


def _nsa_attn_kernel(q_ref, ksl_ref, vsl_ref, kwn_ref, vwn_ref, selb_ref, g_ref, ocmp_ref,
                     o_ref, s0_ref, sw0_ref, m0_ref, s1_ref, sw1_ref, m1_ref, acc_ref):
    i = pl.program_id(2)
    g = pl.program_id(1)
    tq = q_ref.shape[-1]
    nt = s0_ref.shape[1] - 1
    causal, far = _tri_biases(TILE, tq)
    nblk = TILE // SLC_BLOCK
    heads = range(NSA_HPG)
    w = NSA_HPG
    q = lambda p: q_ref[0, p, 0:K_COLS, :]
    banks = ((s0_ref, sw0_ref, m0_ref), (s1_ref, sw1_ref, m1_ref))

    def sel_bias(kt, valid):
        rows = []
        for j in range(nblk):
            row = selb_ref[0, 0, pl.ds(kt * nblk + j, 1), :]
            if valid is not None:
                row = jnp.where(valid, row, NEG)
            rows.append(jnp.broadcast_to(row, (SLC_BLOCK, tq)))
        return jnp.concatenate(rows, axis=0)

    def k_rows(ref, start, n):
        return ref[0, 0, pl.ds(pl.multiple_of(start * TILE, TILE), n * TILE), :]

    def scores_head(t, b):
        sb, swb, mb = b
        bias = sel_bias(t, None) + causal
        k_diag = k_rows(ksl_ref, t, 1)
        t1 = jnp.maximum(t - 1, 0)
        t2 = jnp.maximum(t - 2, 0)
        bias1 = jnp.where(t >= 1, 0.0, NEG)
        bias2 = far + jnp.where(t >= 2, 0.0, NEG)
        kw0, kw1, kw2 = k_rows(kwn_ref, t, 1), k_rows(kwn_ref, t1, 1), k_rows(kwn_ref, t2, 1)

        def head(p):
            _score_step(k_diag, q(p), bias, sb, p, nt, mb, p, True)
            _score_step(kw0, q(p), causal, swb, p, 0, mb, w + p, True)
            _score_step(kw1, q(p), bias1, swb, p, 1, mb, w + p, False)
            _score_step(kw2, q(p), bias2, swb, p, 2, mb, w + p, False)
        return [functools.partial(head, p) for p in heads]

    def scores_pair(t, c, b):
        bias = jnp.concatenate([sel_bias(2 * c, None), sel_bias(2 * c + 1, 2 * c + 1 < t)], axis=0)
        k2 = k_rows(ksl_ref, 2 * c, 2)
        return [functools.partial(_score_step, k2, q(p), bias, b[0], p, 2 * c, b[2], p, False) for p in heads]

    def values_head(t, b):
        sb, swb, mb = b
        t1 = jnp.maximum(t - 1, 0)
        t2 = jnp.maximum(t - 2, 0)
        v_diag = [vsl_ref[0, 0, t]]
        v_win = [vwn_ref[0, 0, t], vwn_ref[0, 0, t1], vwn_ref[0, 0, t2]]

        def head(p):
            _value_step(v_diag, sb, p, nt, mb, p, acc_ref, p, True)
            _value_step(v_win, swb, p, 0, mb, w + p, acc_ref, w + p, True)
        return [functools.partial(head, p) for p in heads]

    def values_pair(c, b):
        v2 = [vsl_ref[0, 0, 2 * c], vsl_ref[0, 0, 2 * c + 1]]
        return [functools.partial(_value_step, v2, b[0], p, 2 * c, b[2], p, acc_ref, p, False) for p in heads]

    def finish():
        for p in heads:
            row = g * NSA_HPG + p
            g_slc = g_ref[0, pl.ds(NSA_HEADS + row, 1), :]
            g_win = g_ref[0, pl.ds(2 * NSA_HEADS + row, 1), :]
            o = ocmp_ref[0, p] + g_slc * _finish(acc_ref, p) + g_win * _finish(acc_ref, w + p)
            o_ref[0, p] = o.astype(o_ref.dtype)

    @pl.when(i == 0)
    def _():
        _interleave(scores_head(i, banks[0]))

    def steady(cur, prv, odd):
        stages = [scores_head(i, cur), values_head(i - 1, prv)]
        if odd:
            stages.insert(1, scores_pair(i, i // 2, cur))
        _interleave(*stages)

        npair = i // 2

        def body2(d, carry):
            _interleave(scores_pair(i, 2 * d, cur), [None] + values_pair(2 * d, prv),
                        scores_pair(i, 2 * d + 1, cur), [None] + values_pair(2 * d + 1, prv))
            return carry
        lax.fori_loop(0, npair // 2, body2, 0)

        def body1(c, carry):
            _interleave(scores_pair(i, c, cur), [None] + values_pair(c, prv))
            return carry
        lax.fori_loop(npair - lax.rem(npair, 2), npair, body1, 0)
        finish()

    for par in (0, 1):
        @pl.when((i >= 1) & (i < nt) & (lax.rem(i, 2) == par))
        def _():
            steady(banks[par], banks[1 - par], par == 1)

    @pl.when(i == nt)
    def _():
        _interleave(values_head(i - 1, banks[1]))

        def body(c, carry):
            _interleave(values_pair(c, banks[1]))
            return carry
        lax.fori_loop(0, i // 2, body, 0)
        finish()


def _nsa_attn(qa, kaug, vt, selb, gt, ocmp):
    b, nq = qa.shape[:2]
    s = nq * TILE
    n_slc = s // SLC_BLOCK
    assert WINDOW == 2 * TILE and TILE % SLC_BLOCK == 0 and nq % 2 == 0
    kspec = lambda off: pl.BlockSpec((1, 1, s, K_COLS), lambda bi, g, i: (bi, g + off, 0, 0))
    vspec = lambda off: pl.BlockSpec((1, 1, nq, V_ROWS, TILE), lambda bi, g, i: (bi, g + off, 0, 0, 0))
    p1 = lambda i: jnp.minimum(i, nq - 1)
    p2 = lambda i: jnp.maximum(i - 1, 0)
    return pl.pallas_call(
        _nsa_attn_kernel,
        grid=(b, NSA_GROUPS, nq + 1),
        in_specs=[
            pl.BlockSpec((1, None, NSA_HPG, Q_ROWS, TILE), lambda bi, g, i: (bi, p1(i), g, 0, 0)),
            kspec(0), vspec(0), kspec(NSA_GROUPS), vspec(NSA_GROUPS),
            pl.BlockSpec((1, None, 1, n_slc, TILE), lambda bi, g, i: (bi, p1(i), g, 0, 0)),
            pl.BlockSpec((1, None, 32, TILE), lambda bi, g, i: (bi, p2(i), 0, 0)),
            pl.BlockSpec((1, None, NSA_HPG, HEAD_DIM, TILE), lambda bi, g, i: (bi, p2(i), g, 0, 0)),
        ],
        out_specs=pl.BlockSpec((1, None, NSA_HPG, HEAD_DIM, TILE), lambda bi, g, i: (bi, p2(i), g, 0, 0)),
        out_shape=jax.ShapeDtypeStruct((b, nq, NSA_HEADS, HEAD_DIM, TILE), BF16),
        scratch_shapes=2 * [pltpu.VMEM((NSA_HPG, nq + 1, TILE, TILE), F32),
                            pltpu.VMEM((NSA_HPG, 3, TILE, TILE), F32),
                            pltpu.VMEM((2 * NSA_HPG, 1, TILE), F32)]
                       + [pltpu.VMEM((2 * NSA_HPG, V_ROWS, TILE), F32)],
        compiler_params=pltpu.CompilerParams(vmem_limit_bytes=VMEM_LIMIT),
        name="nsa_attn",
    )(qa, kaug, vt, kaug, vt, selb, gt, ocmp)


MOBA_HPS = 4


def _moba_kernel(q_ref, k_ref, v_ref, kbar_ref, o_ref, s0_ref, m0_ref, s1_ref, m1_ref, acc_ref, sb_ref):
    i = pl.program_id(2)
    tq = q_ref.shape[-1]
    nt = s0_ref.shape[1] - 1
    causal, _ = _tri_biases(TILE, tq)
    heads = range(MOBA_HPS)
    q = lambda h: q_ref[0, h, 0:K_COLS, :]
    banks = ((s0_ref, m0_ref), (s1_ref, m1_ref))

    def k_rows(h, start, n):
        return k_ref[0, h, pl.ds(pl.multiple_of(start * TILE, TILE), n * TILE), :]

    def scores_head(t, b):
        def head(h):
            kb_hi, kb_lo = _hi_lo(kbar_ref[0, h])
            gate = (_dot(kb_hi, q_ref[0, h, 0:64, :]) + _dot(kb_hi, q_ref[0, h, 128:192, :])
                    + _dot(kb_lo, q_ref[0, h, 192:256, :]))
            nb = gate.shape[0]
            past = lax.broadcasted_iota(jnp.int32, (nb, tq), 0) < t
            sel = _rank_select(jnp.where(past, gate, NEG), nb, MOBA_TOPK)
            sb_ref[h] = jnp.where(sel & past, 0.0, NEG)
            _score_step(k_rows(h, t, 1), q(h), causal, b[0], h, nt, b[1], h, True)
        return [functools.partial(head, h) for h in heads]

    def scores_pair(c, b):
        def head(h):
            bias = _rows2(sb_ref[h, pl.ds(2 * c, 1), :], sb_ref[h, pl.ds(2 * c + 1, 1), :], tq)
            _score_step(k_rows(h, 2 * c, 2), q(h), bias, b[0], h, 2 * c, b[1], h, False)
        return [functools.partial(head, h) for h in heads]

    def values_head(t, b):
        def head(h):
            _value_step([v_ref[0, h, t]], b[0], h, nt, b[1], h, acc_ref, h, True)
        return [functools.partial(head, h) for h in heads]

    def values_pair(c, b):
        def head(h):
            _value_step([v_ref[0, h, 2 * c], v_ref[0, h, 2 * c + 1]], b[0], h, 2 * c, b[1], h, acc_ref, h, False)
        return [functools.partial(head, h) for h in heads]

    def finish():
        for h in heads:
            o_ref[0, h] = _finish(acc_ref, h).astype(o_ref.dtype)

    @pl.when(i == 0)
    def _():
        _interleave(scores_head(i, banks[0]))

    def steady(cur, prv, odd):
        stages = [scores_head(i, cur), values_head(i - 1, prv)]
        if odd:
            stages.insert(1, scores_pair(i // 2, cur))
        _interleave(*stages)

        npair = i // 2

        def body2(d, carry):
            _interleave(scores_pair(2 * d, cur), [None] + values_pair(2 * d, prv),
                        scores_pair(2 * d + 1, cur), [None] + values_pair(2 * d + 1, prv))
            return carry
        lax.fori_loop(0, npair // 2, body2, 0)

        def body1(c, carry):
            _interleave(scores_pair(c, cur), [None] + values_pair(c, prv))
            return carry
        lax.fori_loop(npair - lax.rem(npair, 2), npair, body1, 0)
        finish()

    for par in (0, 1):
        @pl.when((i >= 1) & (i < nt) & (lax.rem(i, 2) == par))
        def _():
            steady(banks[par], banks[1 - par], par == 1)

    @pl.when(i == nt)
    def _():
        _interleave(values_head(i - 1, banks[1]))

        def body(c, carry):
            _interleave(values_pair(c, banks[1]))
            return carry
        lax.fori_loop(0, i // 2, body, 0)
        finish()


def _moba(qb, kaug, vt, kbar):
    b, nq = qb.shape[:2]
    s = nq * TILE
    nb = kbar.shape[2]
    assert nq % 2 == 0 and nb >= nq
    hps = MOBA_HPS
    off = 2 * NSA_GROUPS // hps
    p1 = lambda i: jnp.minimum(i, nq - 1)
    p2 = lambda i: jnp.maximum(i - 1, 0)
    return pl.pallas_call(
        _moba_kernel,
        grid=(b, MOBA_HEADS // hps, nq + 1),
        in_specs=[
            pl.BlockSpec((1, None, hps, Q_ROWS, TILE), lambda bi, h, i: (bi, p1(i), h, 0, 0)),
            pl.BlockSpec((1, hps, s, K_COLS), lambda bi, h, i: (bi, h + off, 0, 0)),
            pl.BlockSpec((1, hps, nq, V_ROWS, TILE), lambda bi, h, i: (bi, h + off, 0, 0, 0)),
            pl.BlockSpec((1, hps, nb, HEAD_DIM), lambda bi, h, i: (bi, h, 0, 0)),
        ],
        out_specs=pl.BlockSpec((1, None, hps, HEAD_DIM, TILE), lambda bi, h, i: (bi, p2(i), h, 0, 0)),
        out_shape=jax.ShapeDtypeStruct((b, nq, MOBA_HEADS, HEAD_DIM, TILE), BF16),
        scratch_shapes=2 * [pltpu.VMEM((hps, nq + 1, TILE, TILE), F32),
                            pltpu.VMEM((hps, 1, TILE), F32)]
                       + [pltpu.VMEM((hps, V_ROWS, TILE), F32),
                          pltpu.VMEM((hps, nb, TILE), F32)],
        compiler_params=pltpu.CompilerParams(vmem_limit_bytes=VMEM_LIMIT),
        name="moba",
    )(qb, kaug, vt, kbar)


def _out_kernel(oa_ref, ob_ref, sz_ref, sgm_ref, x_ref, wa_ref, wb_ref, wo_ref, g_ref, o_ref):
    d = x_ref.shape[-1]
    w = NSA_HEADS * HEAD_DIM
    tr = x_ref.shape[1]
    oa = oa_ref[0].reshape(w, tr)
    ob = ob_ref[0].reshape(w, tr)
    ya = (oa.astype(F32) * sz_ref[0, 0:w, :].astype(F32)).T.astype(BF16)
    yb = (ob.astype(F32) * sz_ref[0, w:, :].astype(F32)).T.astype(BF16)
    sg = sgm_ref[0].astype(F32)
    m = sg[:, 0:d] * _dot(ya, wa_ref[...]) + sg[:, d:] * _dot(yb, wb_ref[...])
    r = _dot(m.astype(BF16), wo_ref[...])
    ms = jnp.mean(r * r, axis=-1, keepdims=True)
    o_ref[0] = x_ref[0] + r * lax.rsqrt(ms + RMS_EPS) * g_ref[...]


def _output(oa, ob, sz, sgm, x, w_a, w_b, w_o, norm_post):
    b, s, d = x.shape
    nt = s // TILE
    nh, hd = oa.shape[2], oa.shape[3]
    w = nh * hd
    const = lambda *shape: pl.BlockSpec(shape, lambda bi, r: (0,) * len(shape))
    return pl.pallas_call(
        _out_kernel,
        grid=(b, nt),
        in_specs=[
            pl.BlockSpec((1, None, nh, hd, TILE), lambda bi, r: (bi, r, 0, 0, 0)),
            pl.BlockSpec((1, None, nh, hd, TILE), lambda bi, r: (bi, r, 0, 0, 0)),
            pl.BlockSpec((1, None, 2 * w, TILE), lambda bi, r: (bi, r, 0, 0)),
            pl.BlockSpec((1, TILE, 2 * d), lambda bi, r: (bi, r, 0)),
            pl.BlockSpec((1, TILE, d), lambda bi, r: (bi, r, 0)),
            const(w, d), const(w, d), const(d, d), const(1, d),
        ],
        out_specs=pl.BlockSpec((1, TILE, d), lambda bi, r: (bi, r, 0)),
        out_shape=jax.ShapeDtypeStruct((b, s, d), F32),
        compiler_params=pltpu.CompilerParams(vmem_limit_bytes=VMEM_LIMIT),
        name="out_proj",
    )(oa, ob, sz, sgm, x, w_a.astype(BF16), w_b.astype(BF16), w_o.astype(BF16), norm_post.reshape(1, d))


def _layer(x, norm_pre, w_in, pe_k, w1_k, w2_k, pe_v, w1_v, w2_v, w_a, w_b, w_o, norm_post):
    b, s, d = x.shape
    assert s % TILE == 0 and TILE == MOBA_BLOCK
    qa, qb, vt, sz, gt, kcv, kaug, kbar, sgm = _project(x, norm_pre, w_in)
    kc, vc = _compress(kcv, pe_k, w1_k, w2_k, pe_v, w1_v, w2_v)
    ocmp, selb = _nsa_cmp(qa, kc, vc, gt)
    oa = _nsa_attn(qa, kaug, vt, selb, gt, ocmp)
    nt = s // TILE
    kbar = kbar.reshape(b, nt, MOBA_HEADS, HEAD_DIM).transpose(0, 2, 1, 3)
    nb = max(16, nt)
    kbar = jnp.pad(kbar, ((0, 0), (0, 0), (0, nb - nt), (0, 0)))
    ob = _moba(qb, kaug, vt, kbar)
    return _output(oa, ob, sz, sgm, x, w_a, w_b, w_o, norm_post)


def kernel(x, norm_pre, w_in, cmp_pe_k, cmp_w1_k, cmp_w2_k, cmp_pe_v, cmp_w1_v, cmp_w2_v,
           w_branch_a, w_branch_b, w_o, norm_post):
    h = x
    for l in range(norm_pre.shape[0]):
        h = _layer(h, norm_pre[l], w_in[l], cmp_pe_k[l], cmp_w1_k[l], cmp_w2_k[l],
                   cmp_pe_v[l], cmp_w1_v[l], cmp_w2_v[l], w_branch_a[l], w_branch_b[l],
                   w_o[l], norm_post[l])
    return h
```

```python
import functools

import numpy as np
import jax
import jax.numpy as jnp
from jax import lax
from jax.experimental import pallas as pl
from jax.experimental.pallas import tpu as pltpu

F32 = jnp.float32
BF16 = jnp.bfloat16

HEAD_DIM = 64
NSA_HEADS = 8
NSA_GROUPS = 2
NSA_HPG = NSA_HEADS // NSA_GROUPS
CMP_LEN = 32
CMP_STRIDE = 16
CMP_HIDDEN = 2 * HEAD_DIM
SLC_BLOCK = 64
SLC_TOPN = 16
WINDOW = 512
MOBA_HEADS = 8
MOBA_BLOCK = 256
MOBA_TOPK = 3
TOTAL_HEADS = NSA_HEADS + MOBA_HEADS
RMS_EPS = 1e-6

TILE = 256
V_ROWS = 80
Q_ROWS = 256
K_COLS = 128
NEG = -1e30
LOG2E = float(np.log2(np.e))
QSCALE = float(HEAD_DIM ** -0.5) * LOG2E
VMEM_LIMIT = 56 * 1024 * 1024


def _split3(v):
    v = np.asarray(v, np.float32)
    rnd = lambda a: a.astype(BF16).astype(np.float32)
    hi = rnd(v)
    mid = rnd(v - hi)
    lo = rnd(v - hi - mid)
    return hi, mid, lo


def _alibi_slopes():
    s = (2.0 ** (-8.0 * np.arange(1, TOTAL_HEADS + 1) / TOTAL_HEADS)).astype(np.float32)
    return np.concatenate([s[0::2], s[1::2]])


def _q_aug_rows(width):
    sl = (_alibi_slopes().astype(np.float64) * LOG2E).astype(np.float32)
    hi, mid, lo = _split3(sl)
    a = np.zeros((TOTAL_HEADS, HEAD_DIM, width), np.float32)
    for r, part in enumerate((hi, hi, mid, mid, lo, lo)):
        a[:, r, :] = part[:, None]
    return jnp.asarray(a, BF16)


def _pos_aug_cols(pos):
    pos = np.asarray(pos, np.int64)
    p_hi = (pos // 64 * 64).astype(np.float32)
    p_lo = (pos % 64).astype(np.float32)
    a = np.zeros((pos.shape[0], HEAD_DIM), np.float32)
    for c in range(3):
        a[:, 2 * c] = p_hi
        a[:, 2 * c + 1] = p_lo
    return jnp.asarray(a)


def _sigmoid(x):
    return 1.0 / (1.0 + jnp.exp(-x))


def _hi_lo(x):
    hi = x.astype(BF16)
    lo = (x - hi.astype(F32)).astype(BF16)
    return hi, lo


def _dot(a, b):
    return jnp.dot(a, b, preferred_element_type=F32)


def _dot_nt(a, b):
    return lax.dot_general(a, b, (((1,), (1,)), ((), ())), preferred_element_type=F32)


def _dot3(a, b):
    a_hi, a_lo = _hi_lo(a)
    b_hi, b_lo = _hi_lo(b)
    return _dot(a_hi, b_hi) + _dot(a_hi, b_lo) + _dot(a_lo, b_hi)


N_VT = 12
N_KA = 12
CT_Q = 0
CT_V = 1024
CT_Z = CT_V + N_VT * HEAD_DIM
CT_G = CT_Z + 1024
CT_ROWS = CT_G + 32
CN_KCV = 0
CN_KA = 256
CN_MERGE = CN_KA + N_KA * HEAD_DIM


def _proj_kernel(x_ref, g_ref, wt_ref, wn_ref, qaug_ref, pos_ref,
                 qa_ref, qb_ref, vt_ref, sz_ref, gt_ref, kcv_ref, kaug_ref, kbar_ref):
    x = x_ref[0]
    tr = x.shape[0]
    ms = jnp.mean(x * x, axis=-1, keepdims=True)
    u = (x * lax.rsqrt(ms + RMS_EPS) * g_ref[...]).astype(BF16)

    acc_t = _dot_nt(wt_ref[...], u)
    for h in range(TOTAL_HEADS):
        q = acc_t[h * HEAD_DIM:(h + 1) * HEAD_DIM] * QSCALE
        hi, lo = _hi_lo(q)
        ref = qa_ref if h < NSA_HEADS else qb_ref
        hh = h % NSA_HEADS
        ref[0, hh, 0:64, :] = hi
        ref[0, hh, 64:128, :] = qaug_ref[h]
        ref[0, hh, 128:192, :] = lo
        ref[0, hh, 192:256, :] = hi
    ones_rows = (lax.broadcasted_iota(jnp.int32, (V_ROWS - HEAD_DIM, tr), 0) == 0).astype(F32)
    for j in range(N_VT):
        v = acc_t[CT_V + j * HEAD_DIM:CT_V + (j + 1) * HEAD_DIM]
        vt_ref[0, j, 0] = jnp.concatenate([v, ones_rows], axis=0).astype(BF16)
    z = acc_t[CT_Z:CT_G]
    sz_ref[0] = (z * _sigmoid(z)).astype(BF16)
    gt_ref[0] = _sigmoid(acc_t[CT_G:CT_ROWS])

    acc_n = _dot(u, wn_ref[...])
    for j in range(4):
        kcv_ref[0, j] = acc_n[:, CN_KCV + j * HEAD_DIM:CN_KCV + (j + 1) * HEAD_DIM]
    pos = pos_ref[...]
    for j in range(N_KA):
        k = acc_n[:, CN_KA + j * HEAD_DIM:CN_KA + (j + 1) * HEAD_DIM]
        kaug_ref[0, j] = jnp.concatenate([k, pos], axis=1).astype(BF16)
    kb = acc_n[:, CN_KA + 4 * HEAD_DIM:CN_MERGE]
    kbar_ref[0, 0] = jnp.mean(kb, axis=0, keepdims=True)


def _proj_weights(w_in, d_model):
    o_q = 0
    o_kv = o_q + NSA_HEADS * HEAD_DIM
    o_g = o_kv + 6 * NSA_GROUPS * HEAD_DIM
    o_za = o_g + 3 * NSA_HEADS
    o_b = o_za + NSA_HEADS * HEAD_DIM
    o_zb = o_b + 3 * MOBA_HEADS * HEAD_DIM
    o_m = o_zb + MOBA_HEADS * HEAD_DIM
    kv = lambda j: w_in[:, o_kv + j * 128:o_kv + (j + 1) * 128]
    qb = w_in[:, o_b:o_b + 512]
    kb = w_in[:, o_b + 512:o_b + 1024]
    vb = w_in[:, o_b + 1024:o_b + 1536]
    wt = jnp.concatenate([
        w_in[:, o_q:o_kv], qb, kv(3), kv(5), vb,
        w_in[:, o_za:o_b], w_in[:, o_zb:o_m], w_in[:, o_g:o_za],
        jnp.zeros((d_model, CT_ROWS - CT_G - 3 * NSA_HEADS), w_in.dtype)], axis=1)
    wn = jnp.concatenate([kv(0), kv(1), kv(2), kv(4), kb], axis=1)
    return wt.T.astype(BF16), wn.astype(BF16), w_in[:, o_m:].astype(BF16)


def _project(x, norm_pre, w_in):
    b, s, d = x.shape
    nt = s // TILE
    wt, wn, w_merge = _proj_weights(w_in, d)
    cn = wn.shape[1]
    qaug = _q_aug_rows(TILE)
    pos = _pos_aug_cols(np.arange(s))
    const = lambda *shape: pl.BlockSpec(shape, lambda bi, r: (0,) * len(shape))
    out_shape = (
        jax.ShapeDtypeStruct((b, nt, NSA_HEADS, Q_ROWS, TILE), BF16),
        jax.ShapeDtypeStruct((b, nt, MOBA_HEADS, Q_ROWS, TILE), BF16),
        jax.ShapeDtypeStruct((b, N_VT, nt, V_ROWS, TILE), BF16),
        jax.ShapeDtypeStruct((b, nt, 1024, TILE), BF16),
        jax.ShapeDtypeStruct((b, nt, 32, TILE), F32),
        jax.ShapeDtypeStruct((b, 4, s, HEAD_DIM), F32),
        jax.ShapeDtypeStruct((b, N_KA, s, K_COLS), BF16),
        jax.ShapeDtypeStruct((b, nt, 1, MOBA_HEADS * HEAD_DIM), F32),
    )
    out_specs = (
        pl.BlockSpec((1, None, NSA_HEADS, Q_ROWS, TILE), lambda bi, r: (bi, r, 0, 0, 0)),
        pl.BlockSpec((1, None, MOBA_HEADS, Q_ROWS, TILE), lambda bi, r: (bi, r, 0, 0, 0)),
        pl.BlockSpec((1, N_VT, 1, V_ROWS, TILE), lambda bi, r: (bi, 0, r, 0, 0)),
        pl.BlockSpec((1, None, 1024, TILE), lambda bi, r: (bi, r, 0, 0)),
        pl.BlockSpec((1, None, 32, TILE), lambda bi, r: (bi, r, 0, 0)),
        pl.BlockSpec((1, 4, TILE, HEAD_DIM), lambda bi, r: (bi, 0, r, 0)),
        pl.BlockSpec((1, N_KA, TILE, K_COLS), lambda bi, r: (bi, 0, r, 0)),
        pl.BlockSpec((1, 1, 1, MOBA_HEADS * HEAD_DIM), lambda bi, r: (bi, r, 0, 0)),
    )
    outs = pl.pallas_call(
        _proj_kernel,
        grid=(b, nt),
        in_specs=[
            pl.BlockSpec((1, TILE, d), lambda bi, r: (bi, r, 0)),
            const(1, d),
            const(CT_ROWS, d),
            const(d, cn),
            const(TOTAL_HEADS, HEAD_DIM, TILE),
            pl.BlockSpec((TILE, HEAD_DIM), lambda bi, r: (r, 0)),
        ],
        out_specs=out_specs,
        out_shape=out_shape,
        compiler_params=pltpu.CompilerParams(vmem_limit_bytes=VMEM_LIMIT),
        name="proj",
    )(x, norm_pre.reshape(1, d), wt, wn, qaug, pos)
    return outs + (w_merge,)


def _compress_core(x_ref, pe_ref, w1_ref, w2_ref):
    nr = x_ref.shape[2] // CMP_STRIDE
    r = jnp.concatenate([x_ref[0, 0, pl.ds(l, nr, stride=CMP_STRIDE), :] for l in range(CMP_STRIDE)], axis=1)
    a = _dot3(r + pe_ref[0], w1_ref[0])
    bm = _dot3(r + pe_ref[1], w1_ref[1])
    h = a + pltpu.roll(bm, nr - 1, axis=0)
    h = h * _sigmoid(h)
    c = _dot3(h, w2_ref[...])
    valid = lax.broadcasted_iota(jnp.int32, c.shape, 0) < nr - 1
    return jnp.where(valid, c, 0.0)


def _compress_k_kernel(r_ref, pe_ref, w1_ref, w2_ref, pos_ref, o_ref):
    c = _compress_core(r_ref, pe_ref, w1_ref, w2_ref)
    hi, lo = _hi_lo(c)
    hi = hi.astype(F32)
    o_ref[0, 0] = jnp.concatenate([hi, pos_ref[...], hi, lo.astype(F32)], axis=1).astype(BF16)


def _compress_v_kernel(r_ref, pe_ref, w1_ref, w2_ref, o_ref):
    c = _compress_core(r_ref, pe_ref, w1_ref, w2_ref)
    o_ref[0, 0] = c.T.astype(BF16)


def _compress(kcv, pe_k, w1_k, w2_k, pe_v, w1_v, w2_v):
    b, _, s, hd = kcv.shape
    nr = s // CMP_STRIDE
    half = CMP_STRIDE * hd
    prep = lambda pe, w1: (pe.reshape(2, 1, half), w1.reshape(2, half, CMP_HIDDEN))
    pos = _pos_aug_cols(np.arange(nr) * CMP_STRIDE + CMP_LEN - 1)
    const = lambda *shape: pl.BlockSpec(shape, lambda bi, g: (0,) * len(shape))
    common = [const(2, 1, half), const(2, half, CMP_HIDDEN), const(CMP_HIDDEN, hd)]
    pe, w1 = prep(pe_k, w1_k)
    kc = pl.pallas_call(
        _compress_k_kernel,
        grid=(b, NSA_GROUPS),
        in_specs=[pl.BlockSpec((1, 1, s, hd), lambda bi, g: (bi, g, 0, 0))] + common + [const(nr, hd)],
        out_specs=pl.BlockSpec((1, 1, nr, 4 * hd), lambda bi, g: (bi, g, 0, 0)),
        out_shape=jax.ShapeDtypeStruct((b, NSA_GROUPS, nr, 4 * hd), BF16),
        name="compress_k",
    )(kcv, pe, w1, w2_k, pos)
    pe, w1 = prep(pe_v, w1_v)
    vc = pl.pallas_call(
        _compress_v_kernel,
        grid=(b, NSA_GROUPS),
        in_specs=[pl.BlockSpec((1, 1, s, hd), lambda bi, g: (bi, g + NSA_GROUPS, 0, 0))] + common,
        out_specs=pl.BlockSpec((1, 1, hd, nr), lambda bi, g: (bi, g, 0, 0)),
        out_shape=jax.ShapeDtypeStruct((b, NSA_GROUPS, hd, nr), BF16),
        name="compress_v",
    )(kcv, pe, w1, w2_v)
    return kc, vc


def _rank_select(v, n_rows, topn):
    sub = 8
    ng = n_rows // sub
    groups = [v[g * sub:(g + 1) * sub, :] for g in range(ng)]
    cnt = [jnp.zeros(groups[0].shape, F32) for _ in range(ng)]
    jloc = lax.broadcasted_iota(jnp.int32, groups[0].shape, 0)
    for i in range(n_rows):
        vi = v[i:i + 1, :]
        gi = i // sub
        for g in range(ng):
            if g < gi:
                one = jnp.where(vi > groups[g], 1.0, 0.0)
            elif g > gi:
                one = jnp.where(vi >= groups[g], 1.0, 0.0)
            else:
                ge = jnp.where(vi >= groups[g], 1.0, 0.0)
                gt = jnp.where(vi > groups[g], 1.0, 0.0)
                one = jnp.where(jloc > i - gi * sub, ge, gt)
            cnt[g] = cnt[g] + one
    return jnp.concatenate(cnt, axis=0) < float(topn)


def _nsa_cmp_kernel(q_ref, kc_ref, vc_ref, g_ref, ov_ref, ocmp_ref, selb_ref, s_ref, m_ref, v_ref):
    i = pl.program_id(2)
    g = pl.program_id(1)
    kc = kc_ref[0, 0]
    vc = vc_ref[0, 0]
    nc = kc.shape[0]
    tq = q_ref.shape[-1]
    n_idx = lax.broadcasted_iota(jnp.int32, (nc, tq), 0)
    t = lax.broadcasted_iota(jnp.int32, (nc, tq), 1) + i * tq
    bias = jnp.where(t >= n_idx * CMP_STRIDE + (CMP_LEN - 1), 0.0, NEG)
    any_valid = lax.broadcasted_iota(jnp.int32, (1, tq), 1) + i * tq >= CMP_LEN - 1
    for p in range(NSA_HPG):
        s = _dot(kc, q_ref[0, p]) + bias
        s_ref[p] = s
        m_ref[p] = jnp.max(s, axis=0, keepdims=True)
    psum = jnp.zeros((nc, tq), F32)
    for p in range(NSA_HPG):
        e = jnp.exp2(s_ref[p] - m_ref[p])
        l = jnp.sum(e, axis=0, keepdims=True)
        rinv = jnp.where(any_valid, 1.0 / jnp.maximum(l, 1e-30), 0.0)
        gate = g_ref[0, pl.ds(g * NSA_HPG + p, 1), :]
        ocmp_ref[0, p] = _dot(vc, e.astype(BF16)) * (rinv * gate)
        psum = psum + e * rinv
    p_hi, p_lo = _hi_lo(psum)
    imp = _dot(ov_ref[...], p_hi) + _dot(ov_ref[...], p_lo)
    n_slc = imp.shape[0]
    blk = lax.broadcasted_iota(jnp.int32, (n_slc, tq), 0)
    cur = (lax.broadcasted_iota(jnp.int32, (n_slc, tq), 1) + i * tq) // SLC_BLOCK
    forced = (blk == 0) | (blk == cur) | (blk == cur - 1)
    v = jnp.where(forced, 3e38, imp)
    v_ref[...] = jnp.where(blk <= cur, v, NEG)

    sub = 8
    rows_per_tile = tq // SLC_BLOCK
    for ng in range(1, n_slc // sub + 1):
        lo_tile = -(-((ng - 1) * sub + 1) // rows_per_tile) - 1
        hi_tile = (ng * sub) // rows_per_tile - 1
        if hi_tile < max(lo_tile, 0):
            continue

        @pl.when((i >= lo_tile) & (i <= hi_tile))
        def _(ng=ng):
            n = ng * sub
            v = v_ref[0:n, :]
            sel = _rank_select(v, n, SLC_TOPN)
            selb_ref[0, 0, 0:n, :] = jnp.where(sel & (v > 0.5 * NEG), 0.0, NEG)
            if n < n_slc:
                selb_ref[0, 0, n:, :] = jnp.full((n_slc - n, tq), NEG, F32)


def _nsa_cmp(qa, kc, vc, gt):
    b, nq = qa.shape[:2]
    s = nq * TILE
    nc = kc.shape[2]
    n_slc = s // SLC_BLOCK
    cs = np.arange(nc) * CMP_STRIDE
    ss = np.arange(n_slc) * SLC_BLOCK
    ov = ((cs[None, :] <= ss[:, None] + SLC_BLOCK - 1) & (cs[None, :] + CMP_LEN - 1 >= ss[:, None]))
    ov[:, nc - 1] = False
    ov = jnp.asarray(ov.astype(np.float32), BF16)
    return pl.pallas_call(
        _nsa_cmp_kernel,
        grid=(b, NSA_GROUPS, nq),
        in_specs=[
            pl.BlockSpec((1, None, NSA_HPG, Q_ROWS, TILE), lambda bi, g, i: (bi, i, g, 0, 0)),
            pl.BlockSpec((1, 1, nc, 4 * HEAD_DIM), lambda bi, g, i: (bi, g, 0, 0)),
            pl.BlockSpec((1, 1, HEAD_DIM, nc), lambda bi, g, i: (bi, g, 0, 0)),
            pl.BlockSpec((1, None, 32, TILE), lambda bi, g, i: (bi, i, 0, 0)),
            pl.BlockSpec((n_slc, nc), lambda bi, g, i: (0, 0)),
        ],
        out_specs=(
            pl.BlockSpec((1, None, NSA_HPG, HEAD_DIM, TILE), lambda bi, g, i: (bi, i, g, 0, 0)),
            pl.BlockSpec((1, None, 1, n_slc, TILE), lambda bi, g, i: (bi, i, g, 0, 0)),
        ),
        out_shape=(
            jax.ShapeDtypeStruct((b, nq, NSA_HEADS, HEAD_DIM, TILE), F32),
            jax.ShapeDtypeStruct((b, nq, NSA_GROUPS, n_slc, TILE), F32),
        ),
        scratch_shapes=[pltpu.VMEM((NSA_HPG, nc, TILE), F32),
                        pltpu.VMEM((NSA_HPG, 1, TILE), F32),
                        pltpu.VMEM((n_slc, TILE), F32)],
        name="nsa_cmp",
    )(qa, kc, vc, gt, ov)


def _score_step(k_rows, q, bias, s_ref, h, slot, m_ref, midx, first):
    n = k_rows.shape[0] // TILE
    s = _dot(k_rows, q)
    if bias is not None:
        s = s + bias
    s_ref[h, pl.ds(slot, n)] = s.reshape(n, TILE, s.shape[-1])
    mt = jnp.max(s, axis=0, keepdims=True)
    m_ref[midx] = mt if first else jnp.maximum(m_ref[midx], mt)


def _value_step(v_tiles, s_ref, h, slot, m_ref, midx, acc_ref, aidx, first):
    n = len(v_tiles)
    p = jnp.exp2(s_ref[h, pl.ds(slot, n)] - m_ref[midx]).astype(BF16)
    pv = _dot(v_tiles[0], p[0])
    for j in range(1, n):
        pv = pv + _dot(v_tiles[j], p[j])
    acc_ref[aidx] = pv if first else acc_ref[aidx] + pv


def _interleave(*stages):
    for k in range(max(len(s) for s in stages)):
        for s in stages:
            if k < len(s) and s[k] is not None:
                s[k]()


def _rows2(row0, row1, tq):
    return jnp.concatenate([jnp.broadcast_to(row0, (TILE, tq)), jnp.broadcast_to(row1, (TILE, tq))], axis=0)


def _finish(acc_ref, idx):
    acc = acc_ref[idx]
    return acc[0:HEAD_DIM] / acc[HEAD_DIM:HEAD_DIM + 1]


def _tri_biases(tk, tq):
    r = lax.broadcasted_iota(jnp.int32, (tk, tq), 0)
    c = lax.broadcasted_iota(jnp.int32, (tk, tq), 1)
    causal = jnp.where(r > c, NEG, 0.0)
    far = jnp.where(r > c, 0.0, NEG)
    return causal, far


def _nsa_attn_kernel(q_ref, ksl_ref, vsl_ref, kwn_ref, vwn_ref, selb_ref, g_ref, ocmp_ref,
                     o_ref, s0_ref, sw0_ref, m0_ref, s1_ref, sw1_ref, m1_ref, acc_ref):
    i = pl.program_id(2)
    g = pl.program_id(1)
    tq = q_ref.shape[-1]
    nt = s0_ref.shape[1] - 1
    causal, far = _tri_biases(TILE, tq)
    nblk = TILE // SLC_BLOCK
    heads = range(NSA_HPG)
    w = NSA_HPG
    q = lambda p: q_ref[0, p, 0:K_COLS, :]
    banks = ((s0_ref, sw0_ref, m0_ref), (s1_ref, sw1_ref, m1_ref))

    def sel_bias(kt, valid):
        rows = []
        for j in range(nblk):
            row = selb_ref[0, 0, pl.ds(kt * nblk + j, 1), :]
            if valid is not None:
                row = jnp.where(valid, row, NEG)
            rows.append(jnp.broadcast_to(row, (SLC_BLOCK, tq)))
        return jnp.concatenate(rows, axis=0)

    def k_rows(ref, start, n):
        return ref[0, 0, pl.ds(pl.multiple_of(start * TILE, TILE), n * TILE), :]

    def scores_head(t, b):
        sb, swb, mb = b
        bias = sel_bias(t, None) + causal
        k_diag = k_rows(ksl_ref, t, 1)
        t1 = jnp.maximum(t - 1, 0)
        t2 = jnp.maximum(t - 2, 0)
        bias1 = jnp.where(t >= 1, 0.0, NEG)
        bias2 = far + jnp.where(t >= 2, 0.0, NEG)
        kw0, kw1, kw2 = k_rows(kwn_ref, t, 1), k_rows(kwn_ref, t1, 1), k_rows(kwn_ref, t2, 1)

        def head(p):
            _score_step(k_diag, q(p), bias, sb, p, nt, mb, p, True)
            _score_step(kw0, q(p), causal, swb, p, 0, mb, w + p, True)
            _score_step(kw1, q(p), bias1, swb, p, 1, mb, w + p, False)
            _score_step(kw2, q(p), bias2, swb, p, 2, mb, w + p, False)
        return [functools.partial(head, p) for p in heads]

    def scores_pair(t, c, b):
        bias = jnp.concatenate([sel_bias(2 * c, None), sel_bias(2 * c + 1, 2 * c + 1 < t)], axis=0)
        k2 = k_rows(ksl_ref, 2 * c, 2)
        return [functools.partial(_score_step, k2, q(p), bias, b[0], p, 2 * c, b[2], p, False) for p in heads]

    def values_head(t, b):
        sb, swb, mb = b
        t1 = jnp.maximum(t - 1, 0)
        t2 = jnp.maximum(t - 2, 0)
        v_diag = [vsl_ref[0, 0, t]]
        v_win = [vwn_ref[0, 0, t], vwn_ref[0, 0, t1], vwn_ref[0, 0, t2]]

        def head(p):
            _value_step(v_diag, sb, p, nt, mb, p, acc_ref, p, True)
            _value_step(v_win, swb, p, 0, mb, w + p, acc_ref, w + p, True)
        return [functools.partial(head, p) for p in heads]

    def values_pair(c, b):
        v2 = [vsl_ref[0, 0, 2 * c], vsl_ref[0, 0, 2 * c + 1]]
        return [functools.partial(_value_step, v2, b[0], p, 2 * c, b[2], p, acc_ref, p, False) for p in heads]

    def finish():
        for p in heads:
            row = g * NSA_HPG + p
            g_slc = g_ref[0, pl.ds(NSA_HEADS + row, 1), :]
            g_win = g_ref[0, pl.ds(2 * NSA_HEADS + row, 1), :]
            o = ocmp_ref[0, p] + g_slc * _finish(acc_ref, p) + g_win * _finish(acc_ref, w + p)
            o_ref[0, p] = o.astype(o_ref.dtype)

    @pl.when(i == 0)
    def _():
        _interleave(scores_head(i, banks[0]))

    def steady(cur, prv, odd):
        stages = [scores_head(i, cur), values_head(i - 1, prv)]
        if odd:
            stages.insert(1, scores_pair(i, i // 2, cur))
        _interleave(*stages)

        npair = i // 2

        def body2(d, carry):
            _interleave(scores_pair(i, 2 * d, cur), [None] + values_pair(2 * d, prv),
                        scores_pair(i, 2 * d + 1, cur), [None] + values_pair(2 * d + 1, prv))
            return carry
        lax.fori_loop(0, npair // 2, body2, 0)

        def body1(c, carry):
            _interleave(scores_pair(i, c, cur), [None] + values_pair(c, prv))
            return carry
        lax.fori_loop(npair - lax.rem(npair, 2), npair, body1, 0)
        finish()

    for par in (0, 1):
        @pl.when((i >= 1) & (i < nt) & (lax.rem(i, 2) == par))
        def _():
            steady(banks[par], banks[1 - par], par == 1)

    @pl.when(i == nt)
    def _():
        _interleave(values_head(i - 1, banks[1]))

        def body(c, carry):
            _interleave(values_pair(c, banks[1]))
            return carry
        lax.fori_loop(0, i // 2, body, 0)
        finish()


def _nsa_attn(qa, kaug, vt, selb, gt, ocmp):
    b, nq = qa.shape[:2]
    s = nq * TILE
    n_slc = s // SLC_BLOCK
    assert WINDOW == 2 * TILE and TILE % SLC_BLOCK == 0 and nq % 2 == 0
    kspec = lambda off: pl.BlockSpec((1, 1, s, K_COLS), lambda bi, g, i: (bi, g + off, 0, 0))
    vspec = lambda off: pl.BlockSpec((1, 1, nq, V_ROWS, TILE), lambda bi, g, i: (bi, g + off, 0, 0, 0))
    p1 = lambda i: jnp.minimum(i, nq - 1)
    p2 = lambda i: jnp.maximum(i - 1, 0)
    return pl.pallas_call(
        _nsa_attn_kernel,
        grid=(b, NSA_GROUPS, nq + 1),
        in_specs=[
            pl.BlockSpec((1, None, NSA_HPG, Q_ROWS, TILE), lambda bi, g, i: (bi, p1(i), g, 0, 0)),
            kspec(0), vspec(0), kspec(NSA_GROUPS), vspec(NSA_GROUPS),
            pl.BlockSpec((1, None, 1, n_slc, TILE), lambda bi, g, i: (bi, p1(i), g, 0, 0)),
            pl.BlockSpec((1, None, 32, TILE), lambda bi, g, i: (bi, p2(i), 0, 0)),
            pl.BlockSpec((1, None, NSA_HPG, HEAD_DIM, TILE), lambda bi, g, i: (bi, p2(i), g, 0, 0)),
        ],
        out_specs=pl.BlockSpec((1, None, NSA_HPG, HEAD_DIM, TILE), lambda bi, g, i: (bi, p2(i), g, 0, 0)),
        out_shape=jax.ShapeDtypeStruct((b, nq, NSA_HEADS, HEAD_DIM, TILE), BF16),
        scratch_shapes=2 * [pltpu.VMEM((NSA_HPG, nq + 1, TILE, TILE), F32),
                            pltpu.VMEM((NSA_HPG, 3, TILE, TILE), F32),
                            pltpu.VMEM((2 * NSA_HPG, 1, TILE), F32)]
                       + [pltpu.VMEM((2 * NSA_HPG, V_ROWS, TILE), F32)],
        compiler_params=pltpu.CompilerParams(vmem_limit_bytes=VMEM_LIMIT),
        name="nsa_attn",
    )(qa, kaug, vt, kaug, vt, selb, gt, ocmp)


MOBA_HPS = 4


def _moba_kernel(q_ref, k_ref, v_ref, kbar_ref, o_ref, s0_ref, m0_ref, s1_ref, m1_ref, acc_ref, sb_ref):
    i = pl.program_id(2)
    tq = q_ref.shape[-1]
    nt = s0_ref.shape[1] - 1
    causal, _ = _tri_biases(TILE, tq)
    heads = range(MOBA_HPS)
    q = lambda h: q_ref[0, h, 0:K_COLS, :]
    banks = ((s0_ref, m0_ref), (s1_ref, m1_ref))

    def k_rows(h, start, n):
        return k_ref[0, h, pl.ds(pl.multiple_of(start * TILE, TILE), n * TILE), :]

    def scores_head(t, b):
        def head(h):
            kb_hi, kb_lo = _hi_lo(kbar_ref[0, h])
            gate = (_dot(kb_hi, q_ref[0, h, 0:64, :]) + _dot(kb_hi, q_ref[0, h, 128:192, :])
                    + _dot(kb_lo, q_ref[0, h, 192:256, :]))
            nb = gate.shape[0]
            past = lax.broadcasted_iota(jnp.int32, (nb, tq), 0) < t
            sel = _rank_select(jnp.where(past, gate, NEG), nb, MOBA_TOPK)
            sb_ref[h] = jnp.where(sel & past, 0.0, NEG)
            _score_step(k_rows(h, t, 1), q(h), causal, b[0], h, nt, b[1], h, True)
        return [functools.partial(head, h) for h in heads]

    def scores_pair(c, b):
        def head(h):
            bias = _rows2(sb_ref[h, pl.ds(2 * c, 1), :], sb_ref[h, pl.ds(2 * c + 1, 1), :], tq)
            _score_step(k_rows(h, 2 * c, 2), q(h), bias, b[0], h, 2 * c, b[1], h, False)
        return [functools.partial(head, h) for h in heads]

    def values_head(t, b):
        def head(h):
            _value_step([v_ref[0, h, t]], b[0], h, nt, b[1], h, acc_ref, h, True)
        return [functools.partial(head, h) for h in heads]

    def values_pair(c, b):
        def head(h):
            _value_step([v_ref[0, h, 2 * c], v_ref[0, h, 2 * c + 1]], b[0], h, 2 * c, b[1], h, acc_ref, h, False)
        return [functools.partial(head, h) for h in heads]

    def finish():
        for h in heads:
            o_ref[0, h] = _finish(acc_ref, h).astype(o_ref.dtype)

    @pl.when(i == 0)
    def _():
        _interleave(scores_head(i, banks[0]))

    def steady(cur, prv, odd):
        stages = [scores_head(i, cur), values_head(i - 1, prv)]
        if odd:
            stages.insert(1, scores_pair(i // 2, cur))
        _interleave(*stages)

        npair = i // 2

        def body2(d, carry):
            _interleave(scores_pair(2 * d, cur), [None] + values_pair(2 * d, prv),
                        scores_pair(2 * d + 1, cur), [None] + values_pair(2 * d + 1, prv))
            return carry
        lax.fori_loop(0, npair // 2, body2, 0)

        def body1(c, carry):
            _interleave(scores_pair(c, cur), [None] + values_pair(c, prv))
            return carry
        lax.fori_loop(npair - lax.rem(npair, 2), npair, body1, 0)
        finish()

    for par in (0, 1):
        @pl.when((i >= 1) & (i < nt) & (lax.rem(i, 2) == par))
        def _():
            steady(banks[par], banks[1 - par], par == 1)

    @pl.when(i == nt)
    def _():
        _interleave(values_head(i - 1, banks[1]))

        def body(c, carry):
            _interleave(values_pair(c, banks[1]))
            return carry
        lax.fori_loop(0, i // 2, body, 0)
        finish()


def _moba(qb, kaug, vt, kbar):
    b, nq = qb.shape[:2]
    s = nq * TILE
    nb = kbar.shape[2]
    assert nq % 2 == 0 and nb >= nq
    hps = MOBA_HPS
    off = 2 * NSA_GROUPS // hps
    p1 = lambda i: jnp.minimum(i, nq - 1)
    p2 = lambda i: jnp.maximum(i - 1, 0)
    return pl.pallas_call(
        _moba_kernel,
        grid=(b, MOBA_HEADS // hps, nq + 1),
        in_specs=[
            pl.BlockSpec((1, None, hps, Q_ROWS, TILE), lambda bi, h, i: (bi, p1(i), h, 0, 0)),
            pl.BlockSpec((1, hps, s, K_COLS), lambda bi, h, i: (bi, h + off, 0, 0)),
            pl.BlockSpec((1, hps, nq, V_ROWS, TILE), lambda bi, h, i: (bi, h + off, 0, 0, 0)),
            pl.BlockSpec((1, hps, nb, HEAD_DIM), lambda bi, h, i: (bi, h, 0, 0)),
        ],
        out_specs=pl.BlockSpec((1, None, hps, HEAD_DIM, TILE), lambda bi, h, i: (bi, p2(i), h, 0, 0)),
        out_shape=jax.ShapeDtypeStruct((b, nq, MOBA_HEADS, HEAD_DIM, TILE), BF16),
        scratch_shapes=2 * [pltpu.VMEM((hps, nq + 1, TILE, TILE), F32),
                            pltpu.VMEM((hps, 1, TILE), F32)]
                       + [pltpu.VMEM((hps, V_ROWS, TILE), F32),
                          pltpu.VMEM((hps, nb, TILE), F32)],
        compiler_params=pltpu.CompilerParams(vmem_limit_bytes=VMEM_LIMIT),
        name="moba",
    )(qb, kaug, vt, kbar)


def _out_kernel(oa_ref, ob_ref, sz_ref, x_ref, gpre_ref, wm_ref, wa_ref, wb_ref, wo_ref, g_ref, o_ref):
    d = x_ref.shape[-1]
    w = NSA_HEADS * HEAD_DIM
    x = x_ref[0]
    ms = jnp.mean(x * x, axis=-1, keepdims=True)
    u = (x * lax.rsqrt(ms + RMS_EPS) * gpre_ref[...]).astype(BF16)
    sg = _sigmoid(_dot(u, wm_ref[...]))
    tr = x_ref.shape[1]
    oa = oa_ref[0].reshape(w, tr)
    ob = ob_ref[0].reshape(w, tr)
    ya = (oa.astype(F32) * sz_ref[0, 0:w, :].astype(F32)).T.astype(BF16)
    yb = (ob.astype(F32) * sz_ref[0, w:, :].astype(F32)).T.astype(BF16)
    m = sg[:, 0:d] * _dot(ya, wa_ref[...]) + sg[:, d:] * _dot(yb, wb_ref[...])
    r = _dot(m.astype(BF16), wo_ref[...])
    ms = jnp.mean(r * r, axis=-1, keepdims=True)
    o_ref[0] = x + r * lax.rsqrt(ms + RMS_EPS) * g_ref[...]


def _output(oa, ob, sz, x, norm_pre, w_merge, w_a, w_b, w_o, norm_post):
    b, s, d = x.shape
    nt = s // TILE
    nh, hd = oa.shape[2], oa.shape[3]
    w = nh * hd
    const = lambda *shape: pl.BlockSpec(shape, lambda bi, r: (0,) * len(shape))
    return pl.pallas_call(
        _out_kernel,
        grid=(b, nt),
        in_specs=[
            pl.BlockSpec((1, None, nh, hd, TILE), lambda bi, r: (bi, r, 0, 0, 0)),
            pl.BlockSpec((1, None, nh, hd, TILE), lambda bi, r: (bi, r, 0, 0, 0)),
            pl.BlockSpec((1, None, 2 * w, TILE), lambda bi, r: (bi, r, 0, 0)),
            pl.BlockSpec((1, TILE, d), lambda bi, r: (bi, r, 0)),
            const(1, d), const(d, 2 * d),
            const(w, d), const(w, d), const(d, d), const(1, d),
        ],
        out_specs=pl.BlockSpec((1, TILE, d), lambda bi, r: (bi, r, 0)),
        out_shape=jax.ShapeDtypeStruct((b, s, d), F32),
        compiler_params=pltpu.CompilerParams(vmem_limit_bytes=VMEM_LIMIT),
        name="out_proj",
    )(oa, ob, sz, x, norm_pre.reshape(1, d), w_merge,
      w_a.astype(BF16), w_b.astype(BF16), w_o.astype(BF16), norm_post.reshape(1, d))


def _layer(x, norm_pre, w_in, pe_k, w1_k, w2_k, pe_v, w1_v, w2_v, w_a, w_b, w_o, norm_post):
    b, s, d = x.shape
    assert s % TILE == 0 and TILE == MOBA_BLOCK
    qa, qb, vt, sz, gt, kcv, kaug, kbar, w_merge = _project(x, norm_pre, w_in)
    kc, vc = _compress(kcv, pe_k, w1_k, w2_k, pe_v, w1_v, w2_v)
    ocmp, selb = _nsa_cmp(qa, kc, vc, gt)
    oa = _nsa_attn(qa, kaug, vt, selb, gt, ocmp)
    nt = s // TILE
    kbar = kbar.reshape(b, nt, MOBA_HEADS, HEAD_DIM).transpose(0, 2, 1, 3)
    nb = max(16, nt)
    kbar = jnp.pad(kbar, ((0, 0), (0, 0), (0, nb - nt), (0, 0)))
    ob = _moba(qb, kaug, vt, kbar)
    return _output(oa, ob, sz, x, norm_pre, w_merge, w_a, w_b, w_o, norm_post)


def kernel(x, norm_pre, w_in, cmp_pe_k, cmp_w1_k, cmp_w2_k, cmp_pe_v, cmp_w1_v, cmp_w2_v,
           w_branch_a, w_branch_b, w_o, norm_post):
    h = x
    for l in range(norm_pre.shape[0]):
        h = _layer(h, norm_pre[l], w_in[l], cmp_pe_k[l], cmp_w1_k[l], cmp_w2_k[l],
                   cmp_pe_v[l], cmp_w1_v[l], cmp_w2_v[l], w_branch_a[l], w_branch_b[l],
                   w_o[l], norm_post[l])
    return h
```

```python
import functools

import numpy as np
import jax
import jax.numpy as jnp
from jax import lax
from jax.experimental import pallas as pl
from jax.experimental.pallas import tpu as pltpu

F32 = jnp.float32
BF16 = jnp.bfloat16

HEAD_DIM = 64
NSA_HEADS = 8
NSA_GROUPS = 2
NSA_HPG = NSA_HEADS // NSA_GROUPS
CMP_LEN = 32
CMP_STRIDE = 16
CMP_HIDDEN = 2 * HEAD_DIM
SLC_BLOCK = 64
SLC_TOPN = 16
WINDOW = 512
MOBA_HEADS = 8
MOBA_BLOCK = 256
MOBA_TOPK = 3
TOTAL_HEADS = NSA_HEADS + MOBA_HEADS
RMS_EPS = 1e-6

TILE = 256
V_ROWS = 80
Q_ROWS = 256
K_COLS = 128
NEG = -1e30
LOG2E = float(np.log2(np.e))
QSCALE = float(HEAD_DIM ** -0.5) * LOG2E
V7X_VMEM_BYTES = 64 * 1024 * 1024
VMEM_LIMIT = V7X_VMEM_BYTES * 7 // 8


def _split3(v):
    v = np.asarray(v, np.float32)
    rnd = lambda a: a.astype(BF16).astype(np.float32)
    hi = rnd(v)
    mid = rnd(v - hi)
    lo = rnd(v - hi - mid)
    return hi, mid, lo


def _alibi_slopes():
    s = (2.0 ** (-8.0 * np.arange(1, TOTAL_HEADS + 1) / TOTAL_HEADS)).astype(np.float32)
    return np.concatenate([s[0::2], s[1::2]])


def _q_aug_rows(width):
    sl = (_alibi_slopes().astype(np.float64) * LOG2E).astype(np.float32)
    hi, mid, lo = _split3(sl)
    a = np.zeros((TOTAL_HEADS, HEAD_DIM, width), np.float32)
    for r, part in enumerate((hi, hi, mid, mid, lo, lo)):
        a[:, r, :] = part[:, None]
    return jnp.asarray(a, BF16)


def _pos_aug_cols(pos):
    pos = np.asarray(pos, np.int64)
    p_hi = (pos // 64 * 64).astype(np.float32)
    p_lo = (pos % 64).astype(np.float32)
    a = np.zeros((pos.shape[0], HEAD_DIM), np.float32)
    for c in range(3):
        a[:, 2 * c] = p_hi
        a[:, 2 * c + 1] = p_lo
    return jnp.asarray(a)


def _sigmoid(x):
    return 1.0 / (1.0 + jnp.exp(-x))


def _hi_lo(x):
    hi = x.astype(BF16)
    lo = (x - hi.astype(F32)).astype(BF16)
    return hi, lo


def _dot(a, b):
    return jnp.dot(a, b, preferred_element_type=F32)


def _dot_nt(a, b):
    return lax.dot_general(a, b, (((1,), (1,)), ((), ())), preferred_element_type=F32)


def _dot3(a, b):
    a_hi, a_lo = _hi_lo(a)
    b_hi, b_lo = _hi_lo(b)
    return _dot(a_hi, b_hi) + _dot(a_hi, b_lo) + _dot(a_lo, b_hi)


N_VT = 12
N_KA = 12
CT_Q = 0
CT_V = 1024
CT_Z = CT_V + N_VT * HEAD_DIM
CT_G = CT_Z + 1024
CT_ROWS = CT_G + 32
CN_KCV = 0
CN_KA = 256
CN_MERGE = CN_KA + N_KA * HEAD_DIM


def _proj_kernel(x_ref, g_ref, wt_ref, wn_ref, qaug_ref, pos_ref,
                 qa_ref, qb_ref, vt_ref, sz_ref, gt_ref, kcv_ref, kaug_ref, kbar_ref):
    x = x_ref[0]
    tr = x.shape[0]
    ms = jnp.mean(x * x, axis=-1, keepdims=True)
    u = (x * lax.rsqrt(ms + RMS_EPS) * g_ref[...]).astype(BF16)

    acc_t = _dot_nt(wt_ref[...], u)
    for h in range(TOTAL_HEADS):
        q = acc_t[h * HEAD_DIM:(h + 1) * HEAD_DIM] * QSCALE
        hi, lo = _hi_lo(q)
        ref = qa_ref if h < NSA_HEADS else qb_ref
        hh = h % NSA_HEADS
        ref[0, hh, 0:64, :] = hi
        ref[0, hh, 64:128, :] = qaug_ref[h]
        ref[0, hh, 128:192, :] = lo
        ref[0, hh, 192:256, :] = hi
    ones_rows = (lax.broadcasted_iota(jnp.int32, (V_ROWS - HEAD_DIM, tr), 0) == 0).astype(F32)
    for j in range(N_VT):
        v = acc_t[CT_V + j * HEAD_DIM:CT_V + (j + 1) * HEAD_DIM]
        vt_ref[0, j, 0] = jnp.concatenate([v, ones_rows], axis=0).astype(BF16)
    z = acc_t[CT_Z:CT_G]
    sz_ref[0] = (z * _sigmoid(z)).astype(BF16)
    gt_ref[0] = _sigmoid(acc_t[CT_G:CT_ROWS])

    acc_n = _dot(u, wn_ref[...])
    for j in range(4):
        kcv_ref[0, j] = acc_n[:, CN_KCV + j * HEAD_DIM:CN_KCV + (j + 1) * HEAD_DIM]
    pos = pos_ref[...]
    for j in range(N_KA):
        k = acc_n[:, CN_KA + j * HEAD_DIM:CN_KA + (j + 1) * HEAD_DIM]
        kaug_ref[0, j] = jnp.concatenate([k, pos], axis=1).astype(BF16)
    kb = acc_n[:, CN_KA + 4 * HEAD_DIM:CN_MERGE]
    kbar_ref[0, 0] = jnp.mean(kb, axis=0, keepdims=True)


def _proj_weights(w_in, d_model):
    o_q = 0
    o_kv = o_q + NSA_HEADS * HEAD_DIM
    o_g = o_kv + 6 * NSA_GROUPS * HEAD_DIM
    o_za = o_g + 3 * NSA_HEADS
    o_b = o_za + NSA_HEADS * HEAD_DIM
    o_zb = o_b + 3 * MOBA_HEADS * HEAD_DIM
    o_m = o_zb + MOBA_HEADS * HEAD_DIM
    kv = lambda j: w_in[:, o_kv + j * 128:o_kv + (j + 1) * 128]
    qb = w_in[:, o_b:o_b + 512]
    kb = w_in[:, o_b + 512:o_b + 1024]
    vb = w_in[:, o_b + 1024:o_b + 1536]
    wt = jnp.concatenate([
        w_in[:, o_q:o_kv], qb, kv(3), kv(5), vb,
        w_in[:, o_za:o_b], w_in[:, o_zb:o_m], w_in[:, o_g:o_za],
        jnp.zeros((d_model, CT_ROWS - CT_G - 3 * NSA_HEADS), w_in.dtype)], axis=1)
    wn = jnp.concatenate([kv(0), kv(1), kv(2), kv(4), kb], axis=1)
    return wt.T.astype(BF16), wn.astype(BF16), w_in[:, o_m:].astype(BF16)


def _project(x, norm_pre, w_in):
    b, s, d = x.shape
    nt = s // TILE
    wt, wn, w_merge = _proj_weights(w_in, d)
    cn = wn.shape[1]
    qaug = _q_aug_rows(TILE)
    pos = _pos_aug_cols(np.arange(s))
    const = lambda *shape: pl.BlockSpec(shape, lambda bi, r: (0,) * len(shape))
    out_shape = (
        jax.ShapeDtypeStruct((b, nt, NSA_HEADS, Q_ROWS, TILE), BF16),
        jax.ShapeDtypeStruct((b, nt, MOBA_HEADS, Q_ROWS, TILE), BF16),
        jax.ShapeDtypeStruct((b, N_VT, nt, V_ROWS, TILE), BF16),
        jax.ShapeDtypeStruct((b, nt, 1024, TILE), BF16),
        jax.ShapeDtypeStruct((b, nt, 32, TILE), F32),
        jax.ShapeDtypeStruct((b, 4, s, HEAD_DIM), F32),
        jax.ShapeDtypeStruct((b, N_KA, s, K_COLS), BF16),
        jax.ShapeDtypeStruct((b, nt, 1, MOBA_HEADS * HEAD_DIM), F32),
    )
    out_specs = (
        pl.BlockSpec((1, None, NSA_HEADS, Q_ROWS, TILE), lambda bi, r: (bi, r, 0, 0, 0)),
        pl.BlockSpec((1, None, MOBA_HEADS, Q_ROWS, TILE), lambda bi, r: (bi, r, 0, 0, 0)),
        pl.BlockSpec((1, N_VT, 1, V_ROWS, TILE), lambda bi, r: (bi, 0, r, 0, 0)),
        pl.BlockSpec((1, None, 1024, TILE), lambda bi, r: (bi, r, 0, 0)),
        pl.BlockSpec((1, None, 32, TILE), lambda bi, r: (bi, r, 0, 0)),
        pl.BlockSpec((1, 4, TILE, HEAD_DIM), lambda bi, r: (bi, 0, r, 0)),
        pl.BlockSpec((1, N_KA, TILE, K_COLS), lambda bi, r: (bi, 0, r, 0)),
        pl.BlockSpec((1, 1, 1, MOBA_HEADS * HEAD_DIM), lambda bi, r: (bi, r, 0, 0)),
    )
    outs = pl.pallas_call(
        _proj_kernel,
        grid=(b, nt),
        in_specs=[
            pl.BlockSpec((1, TILE, d), lambda bi, r: (bi, r, 0)),
            const(1, d),
            const(CT_ROWS, d),
            const(d, cn),
            const(TOTAL_HEADS, HEAD_DIM, TILE),
            pl.BlockSpec((TILE, HEAD_DIM), lambda bi, r: (r, 0)),
        ],
        out_specs=out_specs,
        out_shape=out_shape,
        compiler_params=pltpu.CompilerParams(vmem_limit_bytes=VMEM_LIMIT),
        name="proj",
    )(x, norm_pre.reshape(1, d), wt, wn, qaug, pos)
    return outs + (w_merge,)


def _compress_core(x_ref, pe_ref, w1_ref, w2_ref):
    nr = x_ref.shape[2] // CMP_STRIDE
    r = jnp.concatenate([x_ref[0, 0, pl.ds(l, nr, stride=CMP_STRIDE), :] for l in range(CMP_STRIDE)], axis=1)
    a = _dot3(r + pe_ref[0], w1_ref[0])
    bm = _dot3(r + pe_ref[1], w1_ref[1])
    h = a + pltpu.roll(bm, nr - 1, axis=0)
    h = h * _sigmoid(h)
    c = _dot3(h, w2_ref[...])
    valid = lax.broadcasted_iota(jnp.int32, c.shape, 0) < nr - 1
    return jnp.where(valid, c, 0.0)


def _compress_k_kernel(r_ref, pe_ref, w1_ref, w2_ref, pos_ref, o_ref):
    c = _compress_core(r_ref, pe_ref, w1_ref, w2_ref)
    hi, lo = _hi_lo(c)
    hi = hi.astype(F32)
    o_ref[0, 0] = jnp.concatenate([hi, pos_ref[...], hi, lo.astype(F32)], axis=1).astype(BF16)


def _compress_v_kernel(r_ref, pe_ref, w1_ref, w2_ref, o_ref):
    c = _compress_core(r_ref, pe_ref, w1_ref, w2_ref)
    o_ref[0, 0] = c.T.astype(BF16)


def _compress(kcv, pe_k, w1_k, w2_k, pe_v, w1_v, w2_v):
    b, _, s, hd = kcv.shape
    nr = s // CMP_STRIDE
    half = CMP_STRIDE * hd
    prep = lambda pe, w1: (pe.reshape(2, 1, half), w1.reshape(2, half, CMP_HIDDEN))
    pos = _pos_aug_cols(np.arange(nr) * CMP_STRIDE + CMP_LEN - 1)
    const = lambda *shape: pl.BlockSpec(shape, lambda bi, g: (0,) * len(shape))
    common = [const(2, 1, half), const(2, half, CMP_HIDDEN), const(CMP_HIDDEN, hd)]
    pe, w1 = prep(pe_k, w1_k)
    kc = pl.pallas_call(
        _compress_k_kernel,
        grid=(b, NSA_GROUPS),
        in_specs=[pl.BlockSpec((1, 1, s, hd), lambda bi, g: (bi, g, 0, 0))] + common + [const(nr, hd)],
        out_specs=pl.BlockSpec((1, 1, nr, 4 * hd), lambda bi, g: (bi, g, 0, 0)),
        out_shape=jax.ShapeDtypeStruct((b, NSA_GROUPS, nr, 4 * hd), BF16),
        name="compress_k",
    )(kcv, pe, w1, w2_k, pos)
    pe, w1 = prep(pe_v, w1_v)
    vc = pl.pallas_call(
        _compress_v_kernel,
        grid=(b, NSA_GROUPS),
        in_specs=[pl.BlockSpec((1, 1, s, hd), lambda bi, g: (bi, g + NSA_GROUPS, 0, 0))] + common,
        out_specs=pl.BlockSpec((1, 1, hd, nr), lambda bi, g: (bi, g, 0, 0)),
        out_shape=jax.ShapeDtypeStruct((b, NSA_GROUPS, hd, nr), BF16),
        name="compress_v",
    )(kcv, pe, w1, w2_v)
    return kc, vc


def _rank_select(v, n_rows, topn):
    sub = 8
    ng = n_rows // sub
    groups = [v[g * sub:(g + 1) * sub, :] for g in range(ng)]
    cnt = [jnp.zeros(groups[0].shape, F32) for _ in range(ng)]
    jloc = lax.broadcasted_iota(jnp.int32, groups[0].shape, 0)
    for i in range(n_rows):
        vi = v[i:i + 1, :]
        gi = i // sub
        for g in range(ng):
            if g < gi:
                one = jnp.where(vi > groups[g], 1.0, 0.0)
            elif g > gi:
                one = jnp.where(vi >= groups[g], 1.0, 0.0)
            else:
                ge = jnp.where(vi >= groups[g], 1.0, 0.0)
                gt = jnp.where(vi > groups[g], 1.0, 0.0)
                one = jnp.where(jloc > i - gi * sub, ge, gt)
            cnt[g] = cnt[g] + one
    return jnp.concatenate(cnt, axis=0) < float(topn)


def _nsa_cmp_kernel(q_ref, kc_ref, vc_ref, g_ref, ov_ref, ocmp_ref, selb_ref, s_ref, m_ref, v_ref):
    i = pl.program_id(2)
    g = pl.program_id(1)
    nc = kc_ref.shape[2]
    n_slc = ov_ref.shape[0]
    tq = q_ref.shape[-1]
    any_valid = lax.broadcasted_iota(jnp.int32, (1, tq), 1) + i * tq >= CMP_LEN - 1

    def branch(nr):
        kc = kc_ref[0, 0, 0:nr, :]
        vc = vc_ref[0, 0, :, 0:nr]
        ov = ov_ref[:, 0:nr]
        n_idx = lax.broadcasted_iota(jnp.int32, (nr, tq), 0)
        t = lax.broadcasted_iota(jnp.int32, (nr, tq), 1) + i * tq
        bias = jnp.where(t >= n_idx * CMP_STRIDE + (CMP_LEN - 1), 0.0, NEG)
        for p in range(NSA_HPG):
            s = _dot(kc, q_ref[0, p]) + bias
            s_ref[p, 0:nr] = s
            m_ref[p] = jnp.max(s, axis=0, keepdims=True)
        psum = jnp.zeros((nr, tq), F32)
        for p in range(NSA_HPG):
            e = jnp.exp2(s_ref[p, 0:nr] - m_ref[p])
            l = jnp.sum(e, axis=0, keepdims=True)
            rinv = jnp.where(any_valid, 1.0 / jnp.maximum(l, 1e-30), 0.0)
            gate = g_ref[0, pl.ds(g * NSA_HPG + p, 1), :]
            ocmp_ref[0, p] = _dot(vc, e.astype(BF16)) * (rinv * gate)
            psum = psum + e * rinv
        p_hi, p_lo = _hi_lo(psum)
        imp = _dot(ov, p_hi) + _dot(ov, p_lo)
        blk = lax.broadcasted_iota(jnp.int32, (n_slc, tq), 0)
        cur = (lax.broadcasted_iota(jnp.int32, (n_slc, tq), 1) + i * tq) // SLC_BLOCK
        forced = (blk == 0) | (blk == cur) | (blk == cur - 1)
        v = jnp.where(forced, 3e38, imp)
        v_ref[...] = jnp.where(blk <= cur, v, NEG)

    tiles_per_quarter = nc // 4 // (tq // CMP_STRIDE)
    for k in range(1, 5):
        @pl.when((i >= (k - 1) * tiles_per_quarter) & (i < k * tiles_per_quarter))
        def _(k=k):
            branch(k * (nc // 4))

    sub = 8
    rows_per_tile = tq // SLC_BLOCK
    for ng in range(1, n_slc // sub + 1):
        lo_tile = -(-((ng - 1) * sub + 1) // rows_per_tile) - 1
        hi_tile = (ng * sub) // rows_per_tile - 1
        if hi_tile < max(lo_tile, 0):
            continue

        @pl.when((i >= lo_tile) & (i <= hi_tile))
        def _(ng=ng):
            n = ng * sub
            v = v_ref[0:n, :]
            sel = _rank_select(v, n, SLC_TOPN)
            selb_ref[0, 0, 0:n, :] = jnp.where(sel & (v > 0.5 * NEG), 0.0, NEG)
            if n < n_slc:
                selb_ref[0, 0, n:, :] = jnp.full((n_slc - n, tq), NEG, F32)


def _nsa_cmp(qa, kc, vc, gt):
    b, nq = qa.shape[:2]
    s = nq * TILE
    nc = kc.shape[2]
    assert nc % (4 * (TILE // CMP_STRIDE)) == 0
    n_slc = s // SLC_BLOCK
    cs = np.arange(nc) * CMP_STRIDE
    ss = np.arange(n_slc) * SLC_BLOCK
    ov = ((cs[None, :] <= ss[:, None] + SLC_BLOCK - 1) & (cs[None, :] + CMP_LEN - 1 >= ss[:, None]))
    ov[:, nc - 1] = False
    ov = jnp.asarray(ov.astype(np.float32), BF16)
    return pl.pallas_call(
        _nsa_cmp_kernel,
        grid=(b, NSA_GROUPS, nq),
        in_specs=[
            pl.BlockSpec((1, None, NSA_HPG, Q_ROWS, TILE), lambda bi, g, i: (bi, i, g, 0, 0)),
            pl.BlockSpec((1, 1, nc, 4 * HEAD_DIM), lambda bi, g, i: (bi, g, 0, 0)),
            pl.BlockSpec((1, 1, HEAD_DIM, nc), lambda bi, g, i: (bi, g, 0, 0)),
            pl.BlockSpec((1, None, 32, TILE), lambda bi, g, i: (bi, i, 0, 0)),
            pl.BlockSpec((n_slc, nc), lambda bi, g, i: (0, 0)),
        ],
        out_specs=(
            pl.BlockSpec((1, None, NSA_HPG, HEAD_DIM, TILE), lambda bi, g, i: (bi, i, g, 0, 0)),
            pl.BlockSpec((1, None, 1, n_slc, TILE), lambda bi, g, i: (bi, i, g, 0, 0)),
        ),
        out_shape=(
            jax.ShapeDtypeStruct((b, nq, NSA_HEADS, HEAD_DIM, TILE), F32),
            jax.ShapeDtypeStruct((b, nq, NSA_GROUPS, n_slc, TILE), F32),
        ),
        scratch_shapes=[pltpu.VMEM((NSA_HPG, nc, TILE), F32),
                        pltpu.VMEM((NSA_HPG, 1, TILE), F32),
                        pltpu.VMEM((n_slc, TILE), F32)],
        name="nsa_cmp",
    )(qa, kc, vc, gt, ov)


def _score_step(k_rows, q, bias, s_ref, h, slot, m_ref, midx, first):
    n = k_rows.shape[0] // TILE
    s = _dot(k_rows, q)
    if bias is not None:
        s = s + bias
    s_ref[h, pl.ds(slot, n)] = s.reshape(n, TILE, s.shape[-1])
    mt = jnp.max(s, axis=0, keepdims=True)
    m_ref[midx] = mt if first else jnp.maximum(m_ref[midx], mt)


def _value_step(v_tiles, s_ref, h, slot, m_ref, midx, acc_ref, aidx, first):
    n = len(v_tiles)
    p = jnp.exp2(s_ref[h, pl.ds(slot, n)] - m_ref[midx]).astype(BF16)
    pv = _dot(v_tiles[0], p[0])
    for j in range(1, n):
        pv = pv + _dot(v_tiles[j], p[j])
    acc_ref[aidx] = pv if first else acc_ref[aidx] + pv


def _interleave(*stages):
    for k in range(max(len(s) for s in stages)):
        for s in stages:
            if k < len(s) and s[k] is not None:
                s[k]()


def _rows2(row0, row1, tq):
    return jnp.concatenate([jnp.broadcast_to(row0, (TILE, tq)), jnp.broadcast_to(row1, (TILE, tq))], axis=0)


def _finish(acc_ref, idx):
    acc = acc_ref[idx]
    return acc[0:HEAD_DIM] / acc[HEAD_DIM:HEAD_DIM + 1]


def _tri_biases(tk, tq):
    r = lax.broadcasted_iota(jnp.int32, (tk, tq), 0)
    c = lax.broadcasted_iota(jnp.int32, (tk, tq), 1)
    causal = jnp.where(r > c, NEG, 0.0)
    far = jnp.where(r > c, 0.0, NEG)
    return causal, far


def _nsa_attn_kernel(q_ref, ksl_ref, vsl_ref, kwn_ref, vwn_ref, selb_ref, g_ref, ocmp_ref,
                     o_ref, s0_ref, sw0_ref, m0_ref, s1_ref, sw1_ref, m1_ref, acc_ref):
    i = pl.program_id(2)
    g = pl.program_id(1)
    tq = q_ref.shape[-1]
    nt = s0_ref.shape[1] - 1
    causal, far = _tri_biases(TILE, tq)
    nblk = TILE // SLC_BLOCK
    heads = range(NSA_HPG)
    w = NSA_HPG
    q = lambda p: q_ref[0, p, 0:K_COLS, :]
    banks = ((s0_ref, sw0_ref, m0_ref), (s1_ref, sw1_ref, m1_ref))

    def sel_bias(kt, valid):
        rows = []
        for j in range(nblk):
            row = selb_ref[0, 0, pl.ds(kt * nblk + j, 1), :]
            if valid is not None:
                row = jnp.where(valid, row, NEG)
            rows.append(jnp.broadcast_to(row, (SLC_BLOCK, tq)))
        return jnp.concatenate(rows, axis=0)

    def k_rows(ref, start, n):
        return ref[0, 0, pl.ds(pl.multiple_of(start * TILE, TILE), n * TILE), :]

    def scores_head(t, b):
        sb, swb, mb = b
        bias = sel_bias(t, None) + causal
        k_diag = k_rows(ksl_ref, t, 1)
        t1 = jnp.maximum(t - 1, 0)
        t2 = jnp.maximum(t - 2, 0)
        bias1 = jnp.where(t >= 1, 0.0, NEG)
        bias2 = far + jnp.where(t >= 2, 0.0, NEG)
        kw0, kw1, kw2 = k_rows(kwn_ref, t, 1), k_rows(kwn_ref, t1, 1), k_rows(kwn_ref, t2, 1)

        def head(p):
            _score_step(k_diag, q(p), bias, sb, p, nt, mb, p, True)
            _score_step(kw0, q(p), causal, swb, p, 0, mb, w + p, True)
            _score_step(kw1, q(p), bias1, swb, p, 1, mb, w + p, False)
            _score_step(kw2, q(p), bias2, swb, p, 2, mb, w + p, False)
        return [functools.partial(head, p) for p in heads]

    def scores_pair(t, c, b):
        bias = jnp.concatenate([sel_bias(2 * c, None), sel_bias(2 * c + 1, 2 * c + 1 < t)], axis=0)
        k2 = k_rows(ksl_ref, 2 * c, 2)
        return [functools.partial(_score_step, k2, q(p), bias, b[0], p, 2 * c, b[2], p, False) for p in heads]

    def values_head(t, b):
        sb, swb, mb = b
        t1 = jnp.maximum(t - 1, 0)
        t2 = jnp.maximum(t - 2, 0)
        v_diag = [vsl_ref[0, 0, t]]
        v_win = [vwn_ref[0, 0, t], vwn_ref[0, 0, t1], vwn_ref[0, 0, t2]]

        def head(p):
            _value_step(v_diag, sb, p, nt, mb, p, acc_ref, p, True)
            _value_step(v_win, swb, p, 0, mb, w + p, acc_ref, w + p, True)
        return [functools.partial(head, p) for p in heads]

    def values_pair(c, b):
        v2 = [vsl_ref[0, 0, 2 * c], vsl_ref[0, 0, 2 * c + 1]]
        return [functools.partial(_value_step, v2, b[0], p, 2 * c, b[2], p, acc_ref, p, False) for p in heads]

    def finish():
        for p in heads:
            row = g * NSA_HPG + p
            g_slc = g_ref[0, pl.ds(NSA_HEADS + row, 1), :]
            g_win = g_ref[0, pl.ds(2 * NSA_HEADS + row, 1), :]
            o = ocmp_ref[0, p] + g_slc * _finish(acc_ref, p) + g_win * _finish(acc_ref, w + p)
            o_ref[0, p] = o.astype(o_ref.dtype)

    @pl.when(i == 0)
    def _():
        _interleave(scores_head(i, banks[0]))

    def steady(cur, prv, odd):
        stages = [scores_head(i, cur), values_head(i - 1, prv)]
        if odd:
            stages.insert(1, scores_pair(i, i // 2, cur))
        _interleave(*stages)

        npair = i // 2

        def body2(d, carry):
            _interleave(scores_pair(i, 2 * d, cur), [None] + values_pair(2 * d, prv),
                        scores_pair(i, 2 * d + 1, cur), [None] + values_pair(2 * d + 1, prv))
            return carry
        lax.fori_loop(0, npair // 2, body2, 0)

        def body1(c, carry):
            _interleave(scores_pair(i, c, cur), [None] + values_pair(c, prv))
            return carry
        lax.fori_loop(npair - lax.rem(npair, 2), npair, body1, 0)
        finish()

    for par in (0, 1):
        @pl.when((i >= 1) & (i < nt) & (lax.rem(i, 2) == par))
        def _():
            steady(banks[par], banks[1 - par], par == 1)

    @pl.when(i == nt)
    def _():
        _interleave(values_head(i - 1, banks[1]))

        def body(c, carry):
            _interleave(values_pair(c, banks[1]))
            return carry
        lax.fori_loop(0, i // 2, body, 0)
        finish()


def _nsa_attn(qa, kaug, vt, selb, gt, ocmp):
    b, nq = qa.shape[:2]
    s = nq * TILE
    n_slc = s // SLC_BLOCK
    assert WINDOW == 2 * TILE and TILE % SLC_BLOCK == 0 and nq % 2 == 0
    kspec = lambda off: pl.BlockSpec((1, 1, s, K_COLS), lambda bi, g, i: (bi, g + off, 0, 0))
    vspec = lambda off: pl.BlockSpec((1, 1, nq, V_ROWS, TILE), lambda bi, g, i: (bi, g + off, 0, 0, 0))
    p1 = lambda i: jnp.minimum(i, nq - 1)
    p2 = lambda i: jnp.maximum(i - 1, 0)
    return pl.pallas_call(
        _nsa_attn_kernel,
        grid=(b, NSA_GROUPS, nq + 1),
        in_specs=[
            pl.BlockSpec((1, None, NSA_HPG, K_COLS, TILE), lambda bi, g, i: (bi, p1(i), g, 0, 0)),
            kspec(0), vspec(0), kspec(NSA_GROUPS), vspec(NSA_GROUPS),
            pl.BlockSpec((1, None, 1, n_slc, TILE), lambda bi, g, i: (bi, p1(i), g, 0, 0)),
            pl.BlockSpec((1, None, 32, TILE), lambda bi, g, i: (bi, p2(i), 0, 0)),
            pl.BlockSpec((1, None, NSA_HPG, HEAD_DIM, TILE), lambda bi, g, i: (bi, p2(i), g, 0, 0)),
        ],
        out_specs=pl.BlockSpec((1, None, NSA_HPG, HEAD_DIM, TILE), lambda bi, g, i: (bi, p2(i), g, 0, 0)),
        out_shape=jax.ShapeDtypeStruct((b, nq, NSA_HEADS, HEAD_DIM, TILE), BF16),
        scratch_shapes=2 * [pltpu.VMEM((NSA_HPG, nq + 1, TILE, TILE), F32),
                            pltpu.VMEM((NSA_HPG, 3, TILE, TILE), F32),
                            pltpu.VMEM((2 * NSA_HPG, 1, TILE), F32)]
                       + [pltpu.VMEM((2 * NSA_HPG, V_ROWS, TILE), F32)],
        compiler_params=pltpu.CompilerParams(vmem_limit_bytes=VMEM_LIMIT),
        name="nsa_attn",
    )(qa, kaug, vt, kaug, vt, selb, gt, ocmp)


MOBA_HPS = 4


def _moba_kernel(q_ref, k_ref, v_ref, kbar_ref, o_ref, s0_ref, m0_ref, s1_ref, m1_ref, acc_ref, sb_ref):
    i = pl.program_id(2)
    tq = q_ref.shape[-1]
    nt = s0_ref.shape[1] - 1
    causal, _ = _tri_biases(TILE, tq)
    heads = range(MOBA_HPS)
    q = lambda h: q_ref[0, h, 0:K_COLS, :]
    banks = ((s0_ref, m0_ref), (s1_ref, m1_ref))

    def k_rows(h, start, n):
        return k_ref[0, h, pl.ds(pl.multiple_of(start * TILE, TILE), n * TILE), :]

    def scores_head(t, b):
        def head(h):
            kb_hi, kb_lo = _hi_lo(kbar_ref[0, h])
            gate = (_dot(kb_hi, q_ref[0, h, 0:64, :]) + _dot(kb_hi, q_ref[0, h, 128:192, :])
                    + _dot(kb_lo, q_ref[0, h, 192:256, :]))
            nb = gate.shape[0]
            past = lax.broadcasted_iota(jnp.int32, (nb, tq), 0) < t
            sel = _rank_select(jnp.where(past, gate, NEG), nb, MOBA_TOPK)
            sb_ref[h] = jnp.where(sel & past, 0.0, NEG)
            _score_step(k_rows(h, t, 1), q(h), causal, b[0], h, nt, b[1], h, True)
        return [functools.partial(head, h) for h in heads]

    def scores_pair(c, b):
        def head(h):
            bias = _rows2(sb_ref[h, pl.ds(2 * c, 1), :], sb_ref[h, pl.ds(2 * c + 1, 1), :], tq)
            _score_step(k_rows(h, 2 * c, 2), q(h), bias, b[0], h, 2 * c, b[1], h, False)
        return [functools.partial(head, h) for h in heads]

    def values_head(t, b):
        def head(h):
            _value_step([v_ref[0, h, t]], b[0], h, nt, b[1], h, acc_ref, h, True)
        return [functools.partial(head, h) for h in heads]

    def values_pair(c, b):
        def head(h):
            _value_step([v_ref[0, h, 2 * c], v_ref[0, h, 2 * c + 1]], b[0], h, 2 * c, b[1], h, acc_ref, h, False)
        return [functools.partial(head, h) for h in heads]

    def finish():
        for h in heads:
            o_ref[0, h] = _finish(acc_ref, h).astype(o_ref.dtype)

    @pl.when(i == 0)
    def _():
        _interleave(scores_head(i, banks[0]))

    def steady(cur, prv, odd):
        stages = [scores_head(i, cur), values_head(i - 1, prv)]
        if odd:
            stages.insert(1, scores_pair(i // 2, cur))
        _interleave(*stages)

        npair = i // 2

        def body2(d, carry):
            _interleave(scores_pair(2 * d, cur), [None] + values_pair(2 * d, prv),
                        scores_pair(2 * d + 1, cur), [None] + values_pair(2 * d + 1, prv))
            return carry
        lax.fori_loop(0, npair // 2, body2, 0)

        def body1(c, carry):
            _interleave(scores_pair(c, cur), [None] + values_pair(c, prv))
            return carry
        lax.fori_loop(npair - lax.rem(npair, 2), npair, body1, 0)
        finish()

    for par in (0, 1):
        @pl.when((i >= 1) & (i < nt) & (lax.rem(i, 2) == par))
        def _():
            steady(banks[par], banks[1 - par], par == 1)

    @pl.when(i == nt)
    def _():
        _interleave(values_head(i - 1, banks[1]))

        def body(c, carry):
            _interleave(values_pair(c, banks[1]))
            return carry
        lax.fori_loop(0, i // 2, body, 0)
        finish()


def _moba(qb, kaug, vt, kbar):
    b, nq = qb.shape[:2]
    s = nq * TILE
    nb = kbar.shape[2]
    assert nq % 2 == 0 and nb >= nq
    hps = MOBA_HPS
    off = 2 * NSA_GROUPS // hps
    p1 = lambda i: jnp.minimum(i, nq - 1)
    p2 = lambda i: jnp.maximum(i - 1, 0)
    return pl.pallas_call(
        _moba_kernel,
        grid=(b, MOBA_HEADS // hps, nq + 1),
        in_specs=[
            pl.BlockSpec((1, None, hps, Q_ROWS, TILE), lambda bi, h, i: (bi, p1(i), h, 0, 0)),
            pl.BlockSpec((1, hps, s, K_COLS), lambda bi, h, i: (bi, h + off, 0, 0)),
            pl.BlockSpec((1, hps, nq, V_ROWS, TILE), lambda bi, h, i: (bi, h + off, 0, 0, 0)),
            pl.BlockSpec((1, hps, nb, HEAD_DIM), lambda bi, h, i: (bi, h, 0, 0)),
        ],
        out_specs=pl.BlockSpec((1, None, hps, HEAD_DIM, TILE), lambda bi, h, i: (bi, p2(i), h, 0, 0)),
        out_shape=jax.ShapeDtypeStruct((b, nq, MOBA_HEADS, HEAD_DIM, TILE), BF16),
        scratch_shapes=2 * [pltpu.VMEM((hps, nq + 1, TILE, TILE), F32),
                            pltpu.VMEM((hps, 1, TILE), F32)]
                       + [pltpu.VMEM((hps, V_ROWS, TILE), F32),
                          pltpu.VMEM((hps, nb, TILE), F32)],
        compiler_params=pltpu.CompilerParams(vmem_limit_bytes=VMEM_LIMIT),
        name="moba",
    )(qb, kaug, vt, kbar)


def _out_kernel(oa_ref, ob_ref, sz_ref, x_ref, gpre_ref, wm_ref, wa_ref, wb_ref, wo_ref, g_ref, o_ref):
    d = x_ref.shape[-1]
    w = NSA_HEADS * HEAD_DIM
    x = x_ref[0]
    ms = jnp.mean(x * x, axis=-1, keepdims=True)
    u = (x * lax.rsqrt(ms + RMS_EPS) * gpre_ref[...]).astype(BF16)
    sg = _sigmoid(_dot(u, wm_ref[...]))
    tr = x_ref.shape[1]
    oa = oa_ref[0].reshape(w, tr)
    ob = ob_ref[0].reshape(w, tr)
    ya = (oa.astype(F32) * sz_ref[0, 0:w, :].astype(F32)).T.astype(BF16)
    yb = (ob.astype(F32) * sz_ref[0, w:, :].astype(F32)).T.astype(BF16)
    m = sg[:, 0:d] * _dot(ya, wa_ref[...]) + sg[:, d:] * _dot(yb, wb_ref[...])
    r = _dot(m.astype(BF16), wo_ref[...])
    ms = jnp.mean(r * r, axis=-1, keepdims=True)
    o_ref[0] = x + r * lax.rsqrt(ms + RMS_EPS) * g_ref[...]


def _output(oa, ob, sz, x, norm_pre, w_merge, w_a, w_b, w_o, norm_post):
    b, s, d = x.shape
    nt = s // TILE
    nh, hd = oa.shape[2], oa.shape[3]
    w = nh * hd
    const = lambda *shape: pl.BlockSpec(shape, lambda bi, r: (0,) * len(shape))
    return pl.pallas_call(
        _out_kernel,
        grid=(b, nt),
        in_specs=[
            pl.BlockSpec((1, None, nh, hd, TILE), lambda bi, r: (bi, r, 0, 0, 0)),
            pl.BlockSpec((1, None, nh, hd, TILE), lambda bi, r: (bi, r, 0, 0, 0)),
            pl.BlockSpec((1, None, 2 * w, TILE), lambda bi, r: (bi, r, 0, 0)),
            pl.BlockSpec((1, TILE, d), lambda bi, r: (bi, r, 0)),
            const(1, d), const(d, 2 * d),
            const(w, d), const(w, d), const(d, d), const(1, d),
        ],
        out_specs=pl.BlockSpec((1, TILE, d), lambda bi, r: (bi, r, 0)),
        out_shape=jax.ShapeDtypeStruct((b, s, d), F32),
        compiler_params=pltpu.CompilerParams(vmem_limit_bytes=VMEM_LIMIT),
        name="out_proj",
    )(oa, ob, sz, x, norm_pre.reshape(1, d), w_merge,
      w_a.astype(BF16), w_b.astype(BF16), w_o.astype(BF16), norm_post.reshape(1, d))


def _layer(x, norm_pre, w_in, pe_k, w1_k, w2_k, pe_v, w1_v, w2_v, w_a, w_b, w_o, norm_post):
    b, s, d = x.shape
    assert s % TILE == 0 and TILE == MOBA_BLOCK
    qa, qb, vt, sz, gt, kcv, kaug, kbar, w_merge = _project(x, norm_pre, w_in)
    kc, vc = _compress(kcv, pe_k, w1_k, w2_k, pe_v, w1_v, w2_v)
    ocmp, selb = _nsa_cmp(qa, kc, vc, gt)
    oa = _nsa_attn(qa, kaug, vt, selb, gt, ocmp)
    nt = s // TILE
    kbar = kbar.reshape(b, nt, MOBA_HEADS, HEAD_DIM).transpose(0, 2, 1, 3)
    nb = max(16, nt)
    kbar = jnp.pad(kbar, ((0, 0), (0, 0), (0, nb - nt), (0, 0)))
    ob = _moba(qb, kaug, vt, kbar)
    return _output(oa, ob, sz, x, norm_pre, w_merge, w_a, w_b, w_o, norm_post)


def kernel(x, norm_pre, w_in, cmp_pe_k, cmp_w1_k, cmp_w2_k, cmp_pe_v, cmp_w1_v, cmp_w2_v,
           w_branch_a, w_branch_b, w_o, norm_post):
    h = x
    for l in range(norm_pre.shape[0]):
        h = _layer(h, norm_pre[l], w_in[l], cmp_pe_k[l], cmp_w1_k[l], cmp_w2_k[l],
                   cmp_pe_v[l], cmp_w1_v[l], cmp_w2_v[l], w_branch_a[l], w_branch_b[l],
                   w_o[l], norm_post[l])
    return h
```

```python
import functools

import numpy as np
import jax
import jax.numpy as jnp
from jax import lax
from jax.experimental import pallas as pl
from jax.experimental.pallas import tpu as pltpu

F32 = jnp.float32
BF16 = jnp.bfloat16

HEAD_DIM = 64
NSA_HEADS = 8
NSA_GROUPS = 2
NSA_HPG = NSA_HEADS // NSA_GROUPS
CMP_LEN = 32
CMP_STRIDE = 16
CMP_HIDDEN = 2 * HEAD_DIM
SLC_BLOCK = 64
SLC_TOPN = 16
WINDOW = 512
MOBA_HEADS = 8
MOBA_BLOCK = 256
MOBA_TOPK = 3
TOTAL_HEADS = NSA_HEADS + MOBA_HEADS
RMS_EPS = 1e-6

TILE = 256
V_ROWS = 80
Q_ROWS = 256
K_COLS = 128
NEG = -1e30
LOG2E = float(np.log2(np.e))
QSCALE = float(HEAD_DIM ** -0.5) * LOG2E
V7X_VMEM_BYTES = 64 * 1024 * 1024
VMEM_LIMIT = V7X_VMEM_BYTES * 7 // 8


def _split3(v):
    v = np.asarray(v, np.float32)
    rnd = lambda a: a.astype(BF16).astype(np.float32)
    hi = rnd(v)
    mid = rnd(v - hi)
    lo = rnd(v - hi - mid)
    return hi, mid, lo


def _alibi_slopes():
    s = (2.0 ** (-8.0 * np.arange(1, TOTAL_HEADS + 1) / TOTAL_HEADS)).astype(np.float32)
    return np.concatenate([s[0::2], s[1::2]])


def _q_aug_rows(width):
    sl = (_alibi_slopes().astype(np.float64) * LOG2E).astype(np.float32)
    hi, mid, lo = _split3(sl)
    a = np.zeros((TOTAL_HEADS, HEAD_DIM, width), np.float32)
    for r, part in enumerate((hi, hi, mid, mid, lo, lo)):
        a[:, r, :] = part[:, None]
    return jnp.asarray(a, BF16)


def _pos_aug_cols(pos):
    pos = np.asarray(pos, np.int64)
    p_hi = (pos // 64 * 64).astype(np.float32)
    p_lo = (pos % 64).astype(np.float32)
    a = np.zeros((pos.shape[0], HEAD_DIM), np.float32)
    for c in range(3):
        a[:, 2 * c] = p_hi
        a[:, 2 * c + 1] = p_lo
    return jnp.asarray(a)


def _sigmoid(x):
    return 1.0 / (1.0 + jnp.exp(-x))


def _hi_lo(x):
    hi = x.astype(BF16)
    lo = (x - hi.astype(F32)).astype(BF16)
    return hi, lo


def _dot(a, b):
    return jnp.dot(a, b, preferred_element_type=F32)


def _dot_nt(a, b):
    return lax.dot_general(a, b, (((1,), (1,)), ((), ())), preferred_element_type=F32)


def _dot3(a, b):
    a_hi, a_lo = _hi_lo(a)
    b_hi, b_lo = _hi_lo(b)
    return _dot(a_hi, b_hi) + _dot(a_hi, b_lo) + _dot(a_lo, b_hi)


N_VT = 12
N_KA = 12
CT_Q = 0
CT_V = 1024
CT_Z = CT_V + N_VT * HEAD_DIM
CT_G = CT_Z + 1024
CT_ROWS = CT_G + 32
CN_KCV = 0
CN_KA = 256
CN_MERGE = CN_KA + N_KA * HEAD_DIM


def _proj_kernel(x_ref, g_ref, wt_ref, wn_ref, qaug_ref, pos_ref,
                 qa_ref, qb_ref, vt_ref, sz_ref, gt_ref, kcv_ref, kaug_ref, kbar_ref):
    x = x_ref[0]
    tr = x.shape[0]
    ms = jnp.mean(x * x, axis=-1, keepdims=True)
    u = (x * lax.rsqrt(ms + RMS_EPS) * g_ref[...]).astype(BF16)

    acc_t = _dot_nt(wt_ref[...], u)
    for h in range(TOTAL_HEADS):
        q = acc_t[h * HEAD_DIM:(h + 1) * HEAD_DIM] * QSCALE
        hi, lo = _hi_lo(q)
        ref = qa_ref if h < NSA_HEADS else qb_ref
        hh = h % NSA_HEADS
        ref[0, hh, 0:64, :] = hi
        ref[0, hh, 64:128, :] = qaug_ref[h]
        ref[0, hh, 128:192, :] = lo
        ref[0, hh, 192:256, :] = hi
    ones_rows = (lax.broadcasted_iota(jnp.int32, (V_ROWS - HEAD_DIM, tr), 0) == 0).astype(F32)
    for j in range(N_VT):
        v = acc_t[CT_V + j * HEAD_DIM:CT_V + (j + 1) * HEAD_DIM]
        vt_ref[0, j, 0] = jnp.concatenate([v, ones_rows], axis=0).astype(BF16)
    z = acc_t[CT_Z:CT_G]
    sz_ref[0] = (z * _sigmoid(z)).astype(BF16)
    gt_ref[0] = _sigmoid(acc_t[CT_G:CT_ROWS])

    acc_n = _dot(u, wn_ref[...])
    for j in range(4):
        kcv_ref[0, j] = acc_n[:, CN_KCV + j * HEAD_DIM:CN_KCV + (j + 1) * HEAD_DIM]
    pos = pos_ref[...]
    for j in range(N_KA):
        k = acc_n[:, CN_KA + j * HEAD_DIM:CN_KA + (j + 1) * HEAD_DIM]
        kaug_ref[0, j] = jnp.concatenate([k, pos], axis=1).astype(BF16)
    kb = acc_n[:, CN_KA + 4 * HEAD_DIM:CN_MERGE]
    kbar_ref[0, 0] = jnp.mean(kb, axis=0, keepdims=True)


def _proj_weights(w_in, d_model):
    o_q = 0
    o_kv = o_q + NSA_HEADS * HEAD_DIM
    o_g = o_kv + 6 * NSA_GROUPS * HEAD_DIM
    o_za = o_g + 3 * NSA_HEADS
    o_b = o_za + NSA_HEADS * HEAD_DIM
    o_zb = o_b + 3 * MOBA_HEADS * HEAD_DIM
    o_m = o_zb + MOBA_HEADS * HEAD_DIM
    kv = lambda j: w_in[:, o_kv + j * 128:o_kv + (j + 1) * 128]
    qb = w_in[:, o_b:o_b + 512]
    kb = w_in[:, o_b + 512:o_b + 1024]
    vb = w_in[:, o_b + 1024:o_b + 1536]
    wt = jnp.concatenate([
        w_in[:, o_q:o_kv], qb, kv(3), kv(5), vb,
        w_in[:, o_za:o_b], w_in[:, o_zb:o_m], w_in[:, o_g:o_za],
        jnp.zeros((d_model, CT_ROWS - CT_G - 3 * NSA_HEADS), w_in.dtype)], axis=1)
    wn = jnp.concatenate([kv(0), kv(1), kv(2), kv(4), kb], axis=1)
    return wt.T.astype(BF16), wn.astype(BF16), w_in[:, o_m:].astype(BF16)


def _project(x, norm_pre, w_in):
    b, s, d = x.shape
    nt = s // TILE
    wt, wn, w_merge = _proj_weights(w_in, d)
    cn = wn.shape[1]
    qaug = _q_aug_rows(TILE)
    pos = _pos_aug_cols(np.arange(s))
    const = lambda *shape: pl.BlockSpec(shape, lambda bi, r: (0,) * len(shape))
    out_shape = (
        jax.ShapeDtypeStruct((b, nt, NSA_HEADS, Q_ROWS, TILE), BF16),
        jax.ShapeDtypeStruct((b, nt, MOBA_HEADS, Q_ROWS, TILE), BF16),
        jax.ShapeDtypeStruct((b, N_VT, nt, V_ROWS, TILE), BF16),
        jax.ShapeDtypeStruct((b, nt, 1024, TILE), BF16),
        jax.ShapeDtypeStruct((b, nt, 32, TILE), F32),
        jax.ShapeDtypeStruct((b, 4, s, HEAD_DIM), F32),
        jax.ShapeDtypeStruct((b, N_KA, s, K_COLS), BF16),
        jax.ShapeDtypeStruct((b, nt, 1, MOBA_HEADS * HEAD_DIM), F32),
    )
    out_specs = (
        pl.BlockSpec((1, None, NSA_HEADS, Q_ROWS, TILE), lambda bi, r: (bi, r, 0, 0, 0)),
        pl.BlockSpec((1, None, MOBA_HEADS, Q_ROWS, TILE), lambda bi, r: (bi, r, 0, 0, 0)),
        pl.BlockSpec((1, N_VT, 1, V_ROWS, TILE), lambda bi, r: (bi, 0, r, 0, 0)),
        pl.BlockSpec((1, None, 1024, TILE), lambda bi, r: (bi, r, 0, 0)),
        pl.BlockSpec((1, None, 32, TILE), lambda bi, r: (bi, r, 0, 0)),
        pl.BlockSpec((1, 4, TILE, HEAD_DIM), lambda bi, r: (bi, 0, r, 0)),
        pl.BlockSpec((1, N_KA, TILE, K_COLS), lambda bi, r: (bi, 0, r, 0)),
        pl.BlockSpec((1, 1, 1, MOBA_HEADS * HEAD_DIM), lambda bi, r: (bi, r, 0, 0)),
    )
    outs = pl.pallas_call(
        _proj_kernel,
        grid=(b, nt),
        in_specs=[
            pl.BlockSpec((1, TILE, d), lambda bi, r: (bi, r, 0)),
            const(1, d),
            const(CT_ROWS, d),
            const(d, cn),
            const(TOTAL_HEADS, HEAD_DIM, TILE),
            pl.BlockSpec((TILE, HEAD_DIM), lambda bi, r: (r, 0)),
        ],
        out_specs=out_specs,
        out_shape=out_shape,
        compiler_params=pltpu.CompilerParams(vmem_limit_bytes=VMEM_LIMIT),
        name="proj",
    )(x, norm_pre.reshape(1, d), wt, wn, qaug, pos)
    return outs + (w_merge,)


def _compress_core(x_ref, pe_ref, w1_ref, w2_ref):
    nr = x_ref.shape[2] // CMP_STRIDE
    r = jnp.concatenate([x_ref[0, 0, pl.ds(l, nr, stride=CMP_STRIDE), :] for l in range(CMP_STRIDE)], axis=1)
    a = _dot3(r + pe_ref[0], w1_ref[0])
    bm = _dot3(r + pe_ref[1], w1_ref[1])
    h = a + pltpu.roll(bm, nr - 1, axis=0)
    h = h * _sigmoid(h)
    c = _dot3(h, w2_ref[...])
    valid = lax.broadcasted_iota(jnp.int32, c.shape, 0) < nr - 1
    return jnp.where(valid, c, 0.0)


def _compress_k_kernel(r_ref, pe_ref, w1_ref, w2_ref, pos_ref, o_ref):
    c = _compress_core(r_ref, pe_ref, w1_ref, w2_ref)
    hi, lo = _hi_lo(c)
    hi = hi.astype(F32)
    o_ref[0, 0] = jnp.concatenate([hi, pos_ref[...], hi, lo.astype(F32)], axis=1).astype(BF16)


def _compress_v_kernel(r_ref, pe_ref, w1_ref, w2_ref, o_ref):
    c = _compress_core(r_ref, pe_ref, w1_ref, w2_ref)
    o_ref[0, 0] = c.T.astype(BF16)


def _compress(kcv, pe_k, w1_k, w2_k, pe_v, w1_v, w2_v):
    b, _, s, hd = kcv.shape
    nr = s // CMP_STRIDE
    half = CMP_STRIDE * hd
    prep = lambda pe, w1: (pe.reshape(2, 1, half), w1.reshape(2, half, CMP_HIDDEN))
    pos = _pos_aug_cols(np.arange(nr) * CMP_STRIDE + CMP_LEN - 1)
    const = lambda *shape: pl.BlockSpec(shape, lambda bi, g: (0,) * len(shape))
    common = [const(2, 1, half), const(2, half, CMP_HIDDEN), const(CMP_HIDDEN, hd)]
    pe, w1 = prep(pe_k, w1_k)
    kc = pl.pallas_call(
        _compress_k_kernel,
        grid=(b, NSA_GROUPS),
        in_specs=[pl.BlockSpec((1, 1, s, hd), lambda bi, g: (bi, g, 0, 0))] + common + [const(nr, hd)],
        out_specs=pl.BlockSpec((1, 1, nr, 4 * hd), lambda bi, g: (bi, g, 0, 0)),
        out_shape=jax.ShapeDtypeStruct((b, NSA_GROUPS, nr, 4 * hd), BF16),
        name="compress_k",
    )(kcv, pe, w1, w2_k, pos)
    pe, w1 = prep(pe_v, w1_v)
    vc = pl.pallas_call(
        _compress_v_kernel,
        grid=(b, NSA_GROUPS),
        in_specs=[pl.BlockSpec((1, 1, s, hd), lambda bi, g: (bi, g + NSA_GROUPS, 0, 0))] + common,
        out_specs=pl.BlockSpec((1, 1, hd, nr), lambda bi, g: (bi, g, 0, 0)),
        out_shape=jax.ShapeDtypeStruct((b, NSA_GROUPS, hd, nr), BF16),
        name="compress_v",
    )(kcv, pe, w1, w2_v)
    return kc, vc


def _rank_select(v, n_rows, topn):
    sub = 8
    ng = n_rows // sub
    groups = [v[g * sub:(g + 1) * sub, :] for g in range(ng)]
    cnt = [jnp.zeros(groups[0].shape, F32) for _ in range(ng)]
    jloc = lax.broadcasted_iota(jnp.int32, groups[0].shape, 0)
    for i in range(n_rows):
        vi = v[i:i + 1, :]
        gi = i // sub
        for g in range(ng):
            if g < gi:
                one = jnp.where(vi > groups[g], 1.0, 0.0)
            elif g > gi:
                one = jnp.where(vi >= groups[g], 1.0, 0.0)
            else:
                ge = jnp.where(vi >= groups[g], 1.0, 0.0)
                gt = jnp.where(vi > groups[g], 1.0, 0.0)
                one = jnp.where(jloc > i - gi * sub, ge, gt)
            cnt[g] = cnt[g] + one
    return jnp.concatenate(cnt, axis=0) < float(topn)


def _nsa_cmp_kernel(q_ref, kc_ref, vc_ref, g_ref, ov_ref, ocmp_ref, selb_ref, s_ref, m_ref, v_ref):
    g = pl.program_id(1)

    def tile(i, carry):
        _nsa_cmp_tile(i, g, q_ref.at[:, i], kc_ref, vc_ref, g_ref.at[:, i], ov_ref,
                      ocmp_ref.at[:, i], selb_ref.at[:, i], s_ref, m_ref, v_ref)
        return carry
    lax.fori_loop(0, q_ref.shape[1], tile, 0)


def _nsa_cmp_tile(i, g, q_ref, kc_ref, vc_ref, g_ref, ov_ref, ocmp_ref, selb_ref, s_ref, m_ref, v_ref):
    nc = kc_ref.shape[2]
    n_slc = ov_ref.shape[0]
    tq = q_ref.shape[-1]
    any_valid = lax.broadcasted_iota(jnp.int32, (1, tq), 1) + i * tq >= CMP_LEN - 1

    def branch(nr):
        kc = kc_ref[0, 0, 0:nr, :]
        vc = vc_ref[0, 0, :, 0:nr]
        ov = ov_ref[:, 0:nr]
        n_idx = lax.broadcasted_iota(jnp.int32, (nr, tq), 0)
        t = lax.broadcasted_iota(jnp.int32, (nr, tq), 1) + i * tq
        bias = jnp.where(t >= n_idx * CMP_STRIDE + (CMP_LEN - 1), 0.0, NEG)
        for p in range(NSA_HPG):
            s = _dot(kc, q_ref[0, p]) + bias
            s_ref[p, 0:nr] = s
            m_ref[p] = jnp.max(s, axis=0, keepdims=True)
        psum = jnp.zeros((nr, tq), F32)
        for p in range(NSA_HPG):
            e = jnp.exp2(s_ref[p, 0:nr] - m_ref[p])
            l = jnp.sum(e, axis=0, keepdims=True)
            rinv = jnp.where(any_valid, 1.0 / jnp.maximum(l, 1e-30), 0.0)
            gate = g_ref[0, pl.ds(g * NSA_HPG + p, 1), :]
            ocmp_ref[0, p] = _dot(vc, e.astype(BF16)) * (rinv * gate)
            psum = psum + e * rinv
        p_hi, p_lo = _hi_lo(psum)
        imp = _dot(ov, p_hi) + _dot(ov, p_lo)
        blk = lax.broadcasted_iota(jnp.int32, (n_slc, tq), 0)
        cur = (lax.broadcasted_iota(jnp.int32, (n_slc, tq), 1) + i * tq) // SLC_BLOCK
        forced = (blk == 0) | (blk == cur) | (blk == cur - 1)
        v = jnp.where(forced, 3e38, imp)
        v_ref[...] = jnp.where(blk <= cur, v, NEG)

    tiles_per_quarter = nc // 4 // (tq // CMP_STRIDE)
    for k in range(1, 5):
        @pl.when((i >= (k - 1) * tiles_per_quarter) & (i < k * tiles_per_quarter))
        def _(k=k):
            branch(k * (nc // 4))

    sub = 8
    rows_per_tile = tq // SLC_BLOCK
    for ng in range(1, n_slc // sub + 1):
        lo_tile = -(-((ng - 1) * sub + 1) // rows_per_tile) - 1
        hi_tile = (ng * sub) // rows_per_tile - 1
        if hi_tile < max(lo_tile, 0):
            continue

        @pl.when((i >= lo_tile) & (i <= hi_tile))
        def _(ng=ng):
            n = ng * sub
            v = v_ref[0:n, :]
            sel = _rank_select(v, n, SLC_TOPN)
            selb_ref[0, 0, 0:n, :] = jnp.where(sel & (v > 0.5 * NEG), 0.0, NEG)
            if n < n_slc:
                selb_ref[0, 0, n:, :] = jnp.full((n_slc - n, tq), NEG, F32)


def _nsa_cmp(qa, kc, vc, gt):
    b, nq = qa.shape[:2]
    s = nq * TILE
    nc = kc.shape[2]
    assert nc % (4 * (TILE // CMP_STRIDE)) == 0
    n_slc = s // SLC_BLOCK
    cs = np.arange(nc) * CMP_STRIDE
    ss = np.arange(n_slc) * SLC_BLOCK
    ov = ((cs[None, :] <= ss[:, None] + SLC_BLOCK - 1) & (cs[None, :] + CMP_LEN - 1 >= ss[:, None]))
    ov[:, nc - 1] = False
    ov = jnp.asarray(ov.astype(np.float32), BF16)
    return pl.pallas_call(
        _nsa_cmp_kernel,
        grid=(b, NSA_GROUPS),
        in_specs=[
            pl.BlockSpec((1, nq, NSA_HPG, Q_ROWS, TILE), lambda bi, g: (bi, 0, g, 0, 0)),
            pl.BlockSpec((1, 1, nc, 4 * HEAD_DIM), lambda bi, g: (bi, g, 0, 0)),
            pl.BlockSpec((1, 1, HEAD_DIM, nc), lambda bi, g: (bi, g, 0, 0)),
            pl.BlockSpec((1, nq, 32, TILE), lambda bi, g: (bi, 0, 0, 0)),
            pl.BlockSpec((n_slc, nc), lambda bi, g: (0, 0)),
        ],
        out_specs=(
            pl.BlockSpec((1, nq, NSA_HPG, HEAD_DIM, TILE), lambda bi, g: (bi, 0, g, 0, 0)),
            pl.BlockSpec((1, nq, 1, n_slc, TILE), lambda bi, g: (bi, 0, g, 0, 0)),
        ),
        out_shape=(
            jax.ShapeDtypeStruct((b, nq, NSA_HEADS, HEAD_DIM, TILE), F32),
            jax.ShapeDtypeStruct((b, nq, NSA_GROUPS, n_slc, TILE), F32),
        ),
        scratch_shapes=[pltpu.VMEM((NSA_HPG, nc, TILE), F32),
                        pltpu.VMEM((NSA_HPG, 1, TILE), F32),
                        pltpu.VMEM((n_slc, TILE), F32)],
        compiler_params=pltpu.CompilerParams(vmem_limit_bytes=VMEM_LIMIT),
        name="nsa_cmp",
    )(qa, kc, vc, gt, ov)


def _score_step(k_rows, q, bias, s_ref, h, slot, m_ref, midx, first):
    n = k_rows.shape[0] // TILE
    s = _dot(k_rows, q)
    if bias is not None:
        s = s + bias
    s_ref[h, pl.ds(slot, n)] = s.reshape(n, TILE, s.shape[-1])
    mt = jnp.max(s, axis=0, keepdims=True)
    m_ref[midx] = mt if first else jnp.maximum(m_ref[midx], mt)


def _value_step(v_tiles, s_ref, h, slot, m_ref, midx, acc_ref, aidx, first):
    n = len(v_tiles)
    p = jnp.exp2(s_ref[h, pl.ds(slot, n)] - m_ref[midx]).astype(BF16)
    pv = _dot(v_tiles[0], p[0])
    for j in range(1, n):
        pv = pv + _dot(v_tiles[j], p[j])
    acc_ref[aidx] = pv if first else acc_ref[aidx] + pv


def _interleave(*stages):
    for k in range(max(len(s) for s in stages)):
        for s in stages:
            if k < len(s) and s[k] is not None:
                s[k]()


def _rows2(row0, row1, tq):
    return jnp.concatenate([jnp.broadcast_to(row0, (TILE, tq)), jnp.broadcast_to(row1, (TILE, tq))], axis=0)


def _finish(acc_ref, idx):
    acc = acc_ref[idx]
    return acc[0:HEAD_DIM] / acc[HEAD_DIM:HEAD_DIM + 1]


def _tri_biases(tk, tq):
    r = lax.broadcasted_iota(jnp.int32, (tk, tq), 0)
    c = lax.broadcasted_iota(jnp.int32, (tk, tq), 1)
    causal = jnp.where(r > c, NEG, 0.0)
    far = jnp.where(r > c, 0.0, NEG)
    return causal, far


def _nsa_attn_kernel(q_ref, ksl_ref, vsl_ref, kwn_ref, vwn_ref, selb_ref, g_ref, ocmp_ref,
                     o_ref, s0_ref, sw0_ref, m0_ref, s1_ref, sw1_ref, m1_ref, acc_ref):
    i = pl.program_id(2)
    g = pl.program_id(1)
    tq = q_ref.shape[-1]
    nt = s0_ref.shape[1] - 1
    causal, far = _tri_biases(TILE, tq)
    nblk = TILE // SLC_BLOCK
    heads = range(NSA_HPG)
    w = NSA_HPG
    q = lambda p: q_ref[0, p, 0:K_COLS, :]
    banks = ((s0_ref, sw0_ref, m0_ref), (s1_ref, sw1_ref, m1_ref))

    def sel_bias(kt, valid):
        rows = []
        for j in range(nblk):
            row = selb_ref[0, 0, pl.ds(kt * nblk + j, 1), :]
            if valid is not None:
                row = jnp.where(valid, row, NEG)
            rows.append(jnp.broadcast_to(row, (SLC_BLOCK, tq)))
        return jnp.concatenate(rows, axis=0)

    def k_rows(ref, start, n):
        return ref[0, 0, pl.ds(pl.multiple_of(start * TILE, TILE), n * TILE), :]

    def scores_head(t, b):
        sb, swb, mb = b
        bias = sel_bias(t, None) + causal
        k_diag = k_rows(ksl_ref, t, 1)
        t1 = jnp.maximum(t - 1, 0)
        t2 = jnp.maximum(t - 2, 0)
        bias1 = jnp.where(t >= 1, 0.0, NEG)
        bias2 = far + jnp.where(t >= 2, 0.0, NEG)
        kw0, kw1, kw2 = k_rows(kwn_ref, t, 1), k_rows(kwn_ref, t1, 1), k_rows(kwn_ref, t2, 1)

        def head(p):
            _score_step(k_diag, q(p), bias, sb, p, nt, mb, p, True)
            _score_step(kw0, q(p), causal, swb, p, 0, mb, w + p, True)
            _score_step(kw1, q(p), bias1, swb, p, 1, mb, w + p, False)
            _score_step(kw2, q(p), bias2, swb, p, 2, mb, w + p, False)
        return [functools.partial(head, p) for p in heads]

    def scores_pair(t, c, b):
        bias = jnp.concatenate([sel_bias(2 * c, None), sel_bias(2 * c + 1, 2 * c + 1 < t)], axis=0)
        k2 = k_rows(ksl_ref, 2 * c, 2)
        return [functools.partial(_score_step, k2, q(p), bias, b[0], p, 2 * c, b[2], p, False) for p in heads]

    def values_head(t, b):
        sb, swb, mb = b
        t1 = jnp.maximum(t - 1, 0)
        t2 = jnp.maximum(t - 2, 0)
        v_diag = [vsl_ref[0, 0, t]]
        v_win = [vwn_ref[0, 0, t], vwn_ref[0, 0, t1], vwn_ref[0, 0, t2]]

        def head(p):
            _value_step(v_diag, sb, p, nt, mb, p, acc_ref, p, True)
            _value_step(v_win, swb, p, 0, mb, w + p, acc_ref, w + p, True)
        return [functools.partial(head, p) for p in heads]

    def values_pair(c, b):
        v2 = [vsl_ref[0, 0, 2 * c], vsl_ref[0, 0, 2 * c + 1]]
        return [functools.partial(_value_step, v2, b[0], p, 2 * c, b[2], p, acc_ref, p, False) for p in heads]

    def finish():
        for p in heads:
            row = g * NSA_HPG + p
            g_slc = g_ref[0, pl.ds(NSA_HEADS + row, 1), :]
            g_win = g_ref[0, pl.ds(2 * NSA_HEADS + row, 1), :]
            o = ocmp_ref[0, p] + g_slc * _finish(acc_ref, p) + g_win * _finish(acc_ref, w + p)
            o_ref[0, p] = o.astype(o_ref.dtype)

    @pl.when(i == 0)
    def _():
        _interleave(scores_head(i, banks[0]))

    def steady(cur, prv, odd):
        stages = [scores_head(i, cur), values_head(i - 1, prv)]
        if odd:
            stages.insert(1, scores_pair(i, i // 2, cur))
        _interleave(*stages)

        npair = i // 2

        def body2(d, carry):
            _interleave(scores_pair(i, 2 * d, cur), [None] + values_pair(2 * d, prv),
                        scores_pair(i, 2 * d + 1, cur), [None] + values_pair(2 * d + 1, prv))
            return carry
        lax.fori_loop(0, npair // 2, body2, 0)

        def body1(c, carry):
            _interleave(scores_pair(i, c, cur), [None] + values_pair(c, prv))
            return carry
        lax.fori_loop(npair - lax.rem(npair, 2), npair, body1, 0)
        finish()

    for par in (0, 1):
        @pl.when((i >= 1) & (i < nt) & (lax.rem(i, 2) == par))
        def _():
            steady(banks[par], banks[1 - par], par == 1)

    @pl.when(i == nt)
    def _():
        _interleave(values_head(i - 1, banks[1]))

        def body(c, carry):
            _interleave(values_pair(c, banks[1]))
            return carry
        lax.fori_loop(0, i // 2, body, 0)
        finish()


def _nsa_attn(qa, kaug, vt, selb, gt, ocmp):
    b, nq = qa.shape[:2]
    s = nq * TILE
    n_slc = s // SLC_BLOCK
    assert WINDOW == 2 * TILE and TILE % SLC_BLOCK == 0 and nq % 2 == 0
    kspec = lambda off: pl.BlockSpec((1, 1, s, K_COLS), lambda bi, g, i: (bi, g + off, 0, 0))
    vspec = lambda off: pl.BlockSpec((1, 1, nq, V_ROWS, TILE), lambda bi, g, i: (bi, g + off, 0, 0, 0))
    p1 = lambda i: jnp.minimum(i, nq - 1)
    p2 = lambda i: jnp.maximum(i - 1, 0)
    return pl.pallas_call(
        _nsa_attn_kernel,
        grid=(b, NSA_GROUPS, nq + 1),
        in_specs=[
            pl.BlockSpec((1, None, NSA_HPG, K_COLS, TILE), lambda bi, g, i: (bi, p1(i), g, 0, 0)),
            kspec(0), vspec(0), kspec(NSA_GROUPS), vspec(NSA_GROUPS),
            pl.BlockSpec((1, None, 1, n_slc, TILE), lambda bi, g, i: (bi, p1(i), g, 0, 0)),
            pl.BlockSpec((1, None, 32, TILE), lambda bi, g, i: (bi, p2(i), 0, 0)),
            pl.BlockSpec((1, None, NSA_HPG, HEAD_DIM, TILE), lambda bi, g, i: (bi, p2(i), g, 0, 0)),
        ],
        out_specs=pl.BlockSpec((1, None, NSA_HPG, HEAD_DIM, TILE), lambda bi, g, i: (bi, p2(i), g, 0, 0)),
        out_shape=jax.ShapeDtypeStruct((b, nq, NSA_HEADS, HEAD_DIM, TILE), BF16),
        scratch_shapes=2 * [pltpu.VMEM((NSA_HPG, nq + 1, TILE, TILE), F32),
                            pltpu.VMEM((NSA_HPG, 3, TILE, TILE), F32),
                            pltpu.VMEM((2 * NSA_HPG, 1, TILE), F32)]
                       + [pltpu.VMEM((2 * NSA_HPG, V_ROWS, TILE), F32)],
        compiler_params=pltpu.CompilerParams(vmem_limit_bytes=VMEM_LIMIT),
        name="nsa_attn",
    )(qa, kaug, vt, kaug, vt, selb, gt, ocmp)


MOBA_HPS = 4


def _moba_kernel(q_ref, k_ref, v_ref, kbar_ref, o_ref, s0_ref, m0_ref, s1_ref, m1_ref, acc_ref, sb_ref):
    i = pl.program_id(2)
    tq = q_ref.shape[-1]
    nt = s0_ref.shape[1] - 1
    causal, _ = _tri_biases(TILE, tq)
    heads = range(MOBA_HPS)
    q = lambda h: q_ref[0, h, 0:K_COLS, :]
    banks = ((s0_ref, m0_ref), (s1_ref, m1_ref))

    def k_rows(h, start, n):
        return k_ref[0, h, pl.ds(pl.multiple_of(start * TILE, TILE), n * TILE), :]

    def scores_head(t, b):
        def head(h):
            kb_hi, kb_lo = _hi_lo(kbar_ref[0, h])
            gate = (_dot(kb_hi, q_ref[0, h, 0:64, :]) + _dot(kb_hi, q_ref[0, h, 128:192, :])
                    + _dot(kb_lo, q_ref[0, h, 192:256, :]))
            nb = gate.shape[0]
            past = lax.broadcasted_iota(jnp.int32, (nb, tq), 0) < t
            sel = _rank_select(jnp.where(past, gate, NEG), nb, MOBA_TOPK)
            sb_ref[h] = jnp.where(sel & past, 0.0, NEG)
            _score_step(k_rows(h, t, 1), q(h), causal, b[0], h, nt, b[1], h, True)
        return [functools.partial(head, h) for h in heads]

    def scores_pair(c, b):
        def head(h):
            bias = _rows2(sb_ref[h, pl.ds(2 * c, 1), :], sb_ref[h, pl.ds(2 * c + 1, 1), :], tq)
            _score_step(k_rows(h, 2 * c, 2), q(h), bias, b[0], h, 2 * c, b[1], h, False)
        return [functools.partial(head, h) for h in heads]

    def values_head(t, b):
        def head(h):
            _value_step([v_ref[0, h, t]], b[0], h, nt, b[1], h, acc_ref, h, True)
        return [functools.partial(head, h) for h in heads]

    def values_pair(c, b):
        def head(h):
            _value_step([v_ref[0, h, 2 * c], v_ref[0, h, 2 * c + 1]], b[0], h, 2 * c, b[1], h, acc_ref, h, False)
        return [functools.partial(head, h) for h in heads]

    def finish():
        for h in heads:
            o_ref[0, h] = _finish(acc_ref, h).astype(o_ref.dtype)

    @pl.when(i == 0)
    def _():
        _interleave(scores_head(i, banks[0]))

    def steady(cur, prv, odd):
        stages = [scores_head(i, cur), values_head(i - 1, prv)]
        if odd:
            stages.insert(1, scores_pair(i // 2, cur))
        _interleave(*stages)

        npair = i // 2

        def body2(d, carry):
            _interleave(scores_pair(2 * d, cur), [None] + values_pair(2 * d, prv),
                        scores_pair(2 * d + 1, cur), [None] + values_pair(2 * d + 1, prv))
            return carry
        lax.fori_loop(0, npair // 2, body2, 0)

        def body1(c, carry):
            _interleave(scores_pair(c, cur), [None] + values_pair(c, prv))
            return carry
        lax.fori_loop(npair - lax.rem(npair, 2), npair, body1, 0)
        finish()

    for par in (0, 1):
        @pl.when((i >= 1) & (i < nt) & (lax.rem(i, 2) == par))
        def _():
            steady(banks[par], banks[1 - par], par == 1)

    @pl.when(i == nt)
    def _():
        _interleave(values_head(i - 1, banks[1]))

        def body(c, carry):
            _interleave(values_pair(c, banks[1]))
            return carry
        lax.fori_loop(0, i // 2, body, 0)
        finish()


def _moba(qb, kaug, vt, kbar):
    b, nq = qb.shape[:2]
    s = nq * TILE
    nb = kbar.shape[2]
    assert nq % 2 == 0 and nb >= nq
    hps = MOBA_HPS
    off = 2 * NSA_GROUPS // hps
    p1 = lambda i: jnp.minimum(i, nq - 1)
    p2 = lambda i: jnp.maximum(i - 1, 0)
    return pl.pallas_call(
        _moba_kernel,
        grid=(b, MOBA_HEADS // hps, nq + 1),
        in_specs=[
            pl.BlockSpec((1, None, hps, Q_ROWS, TILE), lambda bi, h, i: (bi, p1(i), h, 0, 0)),
            pl.BlockSpec((1, hps, s, K_COLS), lambda bi, h, i: (bi, h + off, 0, 0)),
            pl.BlockSpec((1, hps, nq, V_ROWS, TILE), lambda bi, h, i: (bi, h + off, 0, 0, 0)),
            pl.BlockSpec((1, hps, nb, HEAD_DIM), lambda bi, h, i: (bi, h, 0, 0)),
        ],
        out_specs=pl.BlockSpec((1, None, hps, HEAD_DIM, TILE), lambda bi, h, i: (bi, p2(i), h, 0, 0)),
        out_shape=jax.ShapeDtypeStruct((b, nq, MOBA_HEADS, HEAD_DIM, TILE), BF16),
        scratch_shapes=2 * [pltpu.VMEM((hps, nq + 1, TILE, TILE), F32),
                            pltpu.VMEM((hps, 1, TILE), F32)]
                       + [pltpu.VMEM((hps, V_ROWS, TILE), F32),
                          pltpu.VMEM((hps, nb, TILE), F32)],
        compiler_params=pltpu.CompilerParams(vmem_limit_bytes=VMEM_LIMIT),
        name="moba",
    )(qb, kaug, vt, kbar)


def _out_kernel(oa_ref, ob_ref, sz_ref, x_ref, gpre_ref, wm_ref, wa_ref, wb_ref, wo_ref, g_ref, o_ref):
    d = x_ref.shape[-1]
    w = NSA_HEADS * HEAD_DIM
    x = x_ref[0]
    ms = jnp.mean(x * x, axis=-1, keepdims=True)
    u = (x * lax.rsqrt(ms + RMS_EPS) * gpre_ref[...]).astype(BF16)
    sg = _sigmoid(_dot(u, wm_ref[...]))
    tr = x_ref.shape[1]
    oa = oa_ref[0].reshape(w, tr)
    ob = ob_ref[0].reshape(w, tr)
    ya = (oa.astype(F32) * sz_ref[0, 0:w, :].astype(F32)).T.astype(BF16)
    yb = (ob.astype(F32) * sz_ref[0, w:, :].astype(F32)).T.astype(BF16)
    m = sg[:, 0:d] * _dot(ya, wa_ref[...]) + sg[:, d:] * _dot(yb, wb_ref[...])
    r = _dot(m.astype(BF16), wo_ref[...])
    ms = jnp.mean(r * r, axis=-1, keepdims=True)
    o_ref[0] = x + r * lax.rsqrt(ms + RMS_EPS) * g_ref[...]


def _output(oa, ob, sz, x, norm_pre, w_merge, w_a, w_b, w_o, norm_post):
    b, s, d = x.shape
    nt = s // TILE
    nh, hd = oa.shape[2], oa.shape[3]
    w = nh * hd
    const = lambda *shape: pl.BlockSpec(shape, lambda bi, r: (0,) * len(shape))
    return pl.pallas_call(
        _out_kernel,
        grid=(b, nt),
        in_specs=[
            pl.BlockSpec((1, None, nh, hd, TILE), lambda bi, r: (bi, r, 0, 0, 0)),
            pl.BlockSpec((1, None, nh, hd, TILE), lambda bi, r: (bi, r, 0, 0, 0)),
            pl.BlockSpec((1, None, 2 * w, TILE), lambda bi, r: (bi, r, 0, 0)),
            pl.BlockSpec((1, TILE, d), lambda bi, r: (bi, r, 0)),
            const(1, d), const(d, 2 * d),
            const(w, d), const(w, d), const(d, d), const(1, d),
        ],
        out_specs=pl.BlockSpec((1, TILE, d), lambda bi, r: (bi, r, 0)),
        out_shape=jax.ShapeDtypeStruct((b, s, d), F32),
        compiler_params=pltpu.CompilerParams(vmem_limit_bytes=VMEM_LIMIT),
        name="out_proj",
    )(oa, ob, sz, x, norm_pre.reshape(1, d), w_merge,
      w_a.astype(BF16), w_b.astype(BF16), w_o.astype(BF16), norm_post.reshape(1, d))


def _layer(x, norm_pre, w_in, pe_k, w1_k, w2_k, pe_v, w1_v, w2_v, w_a, w_b, w_o, norm_post):
    b, s, d = x.shape
    assert s % TILE == 0 and TILE == MOBA_BLOCK
    qa, qb, vt, sz, gt, kcv, kaug, kbar, w_merge = _project(x, norm_pre, w_in)
    kc, vc = _compress(kcv, pe_k, w1_k, w2_k, pe_v, w1_v, w2_v)
    ocmp, selb = _nsa_cmp(qa, kc, vc, gt)
    oa = _nsa_attn(qa, kaug, vt, selb, gt, ocmp)
    nt = s // TILE
    kbar = kbar.reshape(b, nt, MOBA_HEADS, HEAD_DIM).transpose(0, 2, 1, 3)
    nb = max(16, nt)
    kbar = jnp.pad(kbar, ((0, 0), (0, 0), (0, nb - nt), (0, 0)))
    ob = _moba(qb, kaug, vt, kbar)
    return _output(oa, ob, sz, x, norm_pre, w_merge, w_a, w_b, w_o, norm_post)


def kernel(x, norm_pre, w_in, cmp_pe_k, cmp_w1_k, cmp_w2_k, cmp_pe_v, cmp_w1_v, cmp_w2_v,
           w_branch_a, w_branch_b, w_o, norm_post):
    h = x
    for l in range(norm_pre.shape[0]):
        h = _layer(h, norm_pre[l], w_in[l], cmp_pe_k[l], cmp_w1_k[l], cmp_w2_k[l],
                   cmp_pe_v[l], cmp_w1_v[l], cmp_w2_v[l], w_branch_a[l], w_branch_b[l],
                   w_o[l], norm_post[l])
    return h
```
